```python
import jax, jax.numpy as jnp
from jax import lax
import numpy as np

D_MODEL = 1024
BATCH = 4
SEQ = 8192
DEPTH = 2
DEC_BATCH = 16
DEC_SEQ = 32
PAST_LEN = 4096

CHUNK = 64
N_A = DEPTH // 2
N_B = DEPTH - N_A
SGU_BLOCK = 128
SGU_GROUPS = 4
SGU_HALF = D_MODEL
SGU_GROUP_DIM = SGU_HALF // SGU_GROUPS
D_FF = 2816
CONV_W = 3
N_HEADS = 16
HEAD_DIM = D_MODEL // N_HEADS
Q_BLOCK = 128
EPS = 1e-6

kernel_name = 'yoco_gmlp_fox_streaming_encoder'


def rms_norm(x, g):
    xf = x.astype(jnp.float32)
    y = xf * lax.rsqrt(jnp.mean(xf * xf, axis=-1, keepdims=True) + EPS)
    return (y * g.astype(jnp.float32)).astype(x.dtype)


def layer_norm(x, g, b):
    xf = x.astype(jnp.float32)
    xc = xf - jnp.mean(xf, axis=-1, keepdims=True)
    y = xc * lax.rsqrt(jnp.mean(xc * xc, axis=-1, keepdims=True) + EPS)
    return (y * g.astype(jnp.float32) + b.astype(jnp.float32)).astype(x.dtype)


def sgu_mask(n):
    c = np.arange(n) // CHUNK
    return c[None, :] <= c[:, None]


def sgu_mixer(hn, w_in, ln_g, ln_b, w_s, b_s, w_out):
    bsz, s, _ = hn.shape
    blk = min(SGU_BLOCK, s)
    z = jax.nn.gelu(hn @ w_in)
    u, v = jnp.split(z, 2, axis=-1)
    v = layer_norm(v, ln_g, ln_b)
    mask = sgu_mask(SGU_BLOCK)[:blk, :blk]
    ws = jnp.where(mask[None], w_s[:, :blk, :blk], jnp.zeros((), w_s.dtype))
    vb = v.reshape(bsz, s // blk, blk, SGU_GROUPS, SGU_GROUP_DIM)
    mixed = jnp.einsum('gij,bnjgc->bnigc', ws, vb) + b_s[:, :blk].T[None, None, :, :, None]
    out = u * mixed.reshape(bsz, s, SGU_HALF)
    return out @ w_out, v


def conv_ffn(hn, conv_state, w_up, conv_w, conv_b, w_down):
    a = hn @ w_up
    s = a.shape[1]
    ext = jnp.concatenate([conv_state.astype(a.dtype), a], axis=1)
    c = sum(ext[:, k:k + s] * conv_w[k] for k in range(CONV_W)) + conv_b
    gate, val = jnp.split(c, 2, axis=-1)
    y = (jax.nn.silu(gate) * val) @ w_down
    return y, ext[:, -(CONV_W - 1):]


def shared_kv(h, kv_norm, w_k, w_v, k_norm_g, w_f, b_f):
    bsz, s, _ = h.shape
    hn = rms_norm(h, kv_norm)
    k = rms_norm((hn @ w_k).reshape(bsz, s, N_HEADS, HEAD_DIM), k_norm_g)
    v = (hn @ w_v).reshape(bsz, s, N_HEADS, HEAD_DIM)
    logf = jax.nn.log_sigmoid((hn @ w_f).astype(jnp.float32) + b_f.astype(jnp.float32))
    return k, v, logf


def fox_attention(q, k, v, cq, ck, q_off):
    bsz, sq, _, _ = q.shape
    sk = k.shape[1]
    blk = min(Q_BLOCK, sq)
    nb = sq // blk
    scale = HEAD_DIM ** -0.5
    kpos = jnp.arange(sk, dtype=jnp.int32)
    ck_t = ck.astype(jnp.float32).transpose(0, 2, 1)[:, :, None, :]

    def one_block(args):
        qb, cqb, qpos = args
        logits = jnp.einsum('bqhd,bkhd->bhqk', qb, k, preferred_element_type=jnp.float32) * scale
        logits = logits + cqb.astype(jnp.float32).transpose(0, 2, 1)[..., None] - ck_t
        mask = kpos[None, :] <= qpos[:, None]
        p = jax.nn.softmax(jnp.where(mask, logits, -jnp.inf), axis=-1)
        return jnp.einsum('bhqk,bkhd->bqhd', p.astype(v.dtype), v)

    qs = q.reshape(bsz, nb, blk, N_HEADS, HEAD_DIM).transpose(1, 0, 2, 3, 4)
    cqs = cq.reshape(bsz, nb, blk, N_HEADS).transpose(1, 0, 2, 3)
    qposs = (q_off + jnp.arange(sq, dtype=jnp.int32)).reshape(nb, blk)
    out = lax.map(one_block, (qs, cqs, qposs))
    return out.transpose(1, 0, 2, 3, 4).reshape(bsz, sq, N_HEADS * HEAD_DIM)


def run_group(x, conv_in, past, w):
    bsz, s, _ = x.shape
    q_off = 0 if past is None else past[0].shape[1]
    h = x
    sgu_rows, conv_rows = [], []
    shared = None
    for layer in range(DEPTH):
        hn = rms_norm(h, w['norm_mix'][layer])
        if layer < N_A:
            mix, v_rows = sgu_mixer(hn, w['a_w_in'][layer], w['a_ln_g'][layer], w['a_ln_b'][layer],
                                    w['a_w_s'][layer], w['a_b_s'][layer], w['a_w_out'][layer])
            sgu_rows.append(v_rows)
        else:
            if shared is None:
                k_new, v_new, lf_new = shared_kv(h, w['kv_norm'], w['w_k'], w['w_v'],
                                                 w['k_norm_g'], w['w_f'], w['b_f'])
                if past is None:
                    k_all, v_all, lf_all = k_new, v_new, lf_new
                else:
                    k_all = jnp.concatenate([past[0].astype(k_new.dtype), k_new], axis=1)
                    v_all = jnp.concatenate([past[1].astype(v_new.dtype), v_new], axis=1)
                    lf_all = jnp.concatenate([past[2].astype(jnp.float32), lf_new], axis=1)
                c_all = jnp.cumsum(lf_all.astype(jnp.float32), axis=1)
                shared = (k_all, v_all, c_all)
            j = layer - N_A
            q = rms_norm((hn @ w['b_w_q'][j]).reshape(bsz, s, N_HEADS, HEAD_DIM), w['q_norm_g'][j])
            att = fox_attention(q, shared[0], shared[1], shared[2][:, q_off:], shared[2], q_off)
            mix = att @ w['b_w_o'][j]
        h = h + mix
        y, conv_new = conv_ffn(rms_norm(h, w['norm_ffn'][layer]), conv_in[layer], w['f_w_up'][layer],
                               w['f_conv_w'][layer], w['f_conv_b'][layer], w['f_w_down'][layer])
        conv_rows.append(conv_new)
        h = h + y
    return h, jnp.stack(sgu_rows), jnp.stack(conv_rows), k_new, v_new, lf_new


def setup_inputs(seed: int = 0) -> dict:
    key = jax.random.key(seed)
    ks = jax.random.split(key, 32)

    def nrm(k, shape, scale=1.0):
        return jax.random.normal(k, shape, jnp.float32) * scale

    hd = N_HEADS * HEAD_DIM
    return {
        'x_prompt': nrm(ks[0], (BATCH, SEQ, D_MODEL)),
        'x_sample': nrm(ks[1], (DEC_BATCH, DEC_SEQ, D_MODEL)),
        'cache_k': nrm(ks[2], (DEC_BATCH, PAST_LEN, N_HEADS, HEAD_DIM)),
        'cache_v': nrm(ks[3], (DEC_BATCH, PAST_LEN, N_HEADS, HEAD_DIM)),
        'cache_logf': jax.nn.log_sigmoid(3.0 + nrm(ks[4], (DEC_BATCH, PAST_LEN, N_HEADS))),
        'cache_ffn_conv': nrm(ks[5], (DEPTH, DEC_BATCH, CONV_W - 1, 2 * D_FF)),
        'norm_mix': 1.0 + nrm(ks[6], (DEPTH, D_MODEL), 0.01),
        'norm_ffn': 1.0 + nrm(ks[7], (DEPTH, D_MODEL), 0.01),
        'a_w_in': nrm(ks[8], (N_A, D_MODEL, 2 * SGU_HALF), D_MODEL ** -0.5),
        'a_ln_g': 1.0 + nrm(ks[9], (N_A, SGU_HALF), 0.01),
        'a_ln_b': nrm(ks[10], (N_A, SGU_HALF), 0.01),
        'a_w_s': nrm(ks[11], (N_A, SGU_GROUPS, SGU_BLOCK, SGU_BLOCK), 0.5 * SGU_BLOCK ** -0.5),
        'a_b_s': 1.0 + nrm(ks[12], (N_A, SGU_GROUPS, SGU_BLOCK), 0.01),
        'a_w_out': nrm(ks[13], (N_A, SGU_HALF, D_MODEL), SGU_HALF ** -0.5),
        'f_w_up': nrm(ks[14], (DEPTH, D_MODEL, 2 * D_FF), D_MODEL ** -0.5),
        'f_conv_w': nrm(ks[15], (DEPTH, CONV_W, 2 * D_FF), CONV_W ** -0.5),
        'f_conv_b': nrm(ks[16], (DEPTH, 2 * D_FF), 0.01),
        'f_w_down': nrm(ks[17], (DEPTH, D_FF, D_MODEL), D_FF ** -0.5),
        'kv_norm': 1.0 + nrm(ks[18], (D_MODEL,), 0.01),
        'w_k': nrm(ks[19], (D_MODEL, hd), D_MODEL ** -0.5),
        'w_v': nrm(ks[20], (D_MODEL, hd), D_MODEL ** -0.5),
        'k_norm_g': 1.0 + nrm(ks[21], (HEAD_DIM,), 0.01),
        'w_f': nrm(ks[22], (D_MODEL, N_HEADS), D_MODEL ** -0.5),
        'b_f': jax.random.uniform(ks[23], (N_HEADS,), jnp.float32, 1.0, 5.0),
        'b_w_q': nrm(ks[24], (N_B, D_MODEL, hd), D_MODEL ** -0.5),
        'q_norm_g': 1.0 + nrm(ks[25], (N_B, HEAD_DIM), 0.01),
        'b_w_o': nrm(ks[26], (N_B, hd, D_MODEL), hd ** -0.5),
    }


def reference(x_prompt, x_sample, cache_k, cache_v, cache_logf, cache_ffn_conv,
              norm_mix, norm_ffn, a_w_in, a_ln_g, a_ln_b, a_w_s, a_b_s, a_w_out,
              f_w_up, f_conv_w, f_conv_b, f_w_down,
              kv_norm, w_k, w_v, k_norm_g, w_f, b_f,
              b_w_q, q_norm_g, b_w_o):
    w = {
        'norm_mix': norm_mix, 'norm_ffn': norm_ffn,
        'a_w_in': a_w_in, 'a_ln_g': a_ln_g, 'a_ln_b': a_ln_b, 'a_w_s': a_w_s, 'a_b_s': a_b_s,
        'a_w_out': a_w_out,
        'f_w_up': f_w_up, 'f_conv_w': f_conv_w, 'f_conv_b': f_conv_b, 'f_w_down': f_w_down,
        'kv_norm': kv_norm, 'w_k': w_k, 'w_v': w_v, 'k_norm_g': k_norm_g, 'w_f': w_f, 'b_f': b_f,
        'b_w_q': b_w_q, 'q_norm_g': q_norm_g, 'b_w_o': b_w_o,
    }
    zero_conv = jnp.zeros((DEPTH, x_prompt.shape[0], CONV_W - 1, 2 * D_FF), x_prompt.dtype)
    y_prompt, _, conv_p, k_p, v_p, lf_p = run_group(x_prompt, zero_conv, None, w)
    y_sample, sgu_v_s, conv_s, k_s, v_s, lf_s = run_group(
        x_sample, cache_ffn_conv, (cache_k, cache_v, cache_logf), w)
    return (y_prompt, y_sample, sgu_v_s, conv_p, conv_s, k_p, v_p, lf_p, k_s, v_s, lf_s)
```

```python
import functools

import jax
import jax.numpy as jnp
import numpy as np
from jax import lax
from jax.experimental import pallas as pl
from jax.experimental.pallas import tpu as pltpu

D_MODEL = 1024
DEPTH = 2
N_A = DEPTH // 2
CHUNK = 64
SGU_BLOCK = 128
SGU_GROUPS = 4
SGU_GROUP_DIM = D_MODEL // SGU_GROUPS
D_FF = 2816
CONV_W = 3
N_HEADS = 16
HEAD_DIM = D_MODEL // N_HEADS
EPS = 1e-6

LANES = 128
HEADS_PER_LANE_TILE = LANES // HEAD_DIM
N_HEAD_PAIRS = N_HEADS // HEADS_PER_LANE_TILE
NEG_BIG = -1e30
VMEM_LIMIT = 56 * 1024 * 1024

F32 = jnp.float32
BF16 = jnp.bfloat16


def _const_spec(shape):
    nd = len(shape)
    return pl.BlockSpec(shape, lambda *_: (0,) * nd, pipeline_mode=pl.Buffered(1))


def _params(semantics):
    return pltpu.CompilerParams(dimension_semantics=semantics, vmem_limit_bytes=VMEM_LIMIT)


def _rms_hat(x):
    return x * lax.rsqrt(jnp.mean(x * x, axis=-1, keepdims=True) + EPS)


def _split2(x):
    hi = x.astype(BF16)
    lo = (x - hi.astype(F32)).astype(BF16)
    return hi, lo


def _split3(x):
    hi = x.astype(BF16)
    r = x - hi.astype(F32)
    mid = r.astype(BF16)
    lo = (r - mid.astype(F32)).astype(BF16)
    return hi, mid, lo


def _dot(a, b):
    return jnp.dot(a, b, preferred_element_type=F32)


def _dot_nt(a, b):
    return lax.dot_general(a, b, (((1,), (1,)), ((), ())), preferred_element_type=F32)


def _sgu_kernel(x_ref, g_ref, win_ref, lng_ref, lnb_ref, ws_ref, bs_ref, wout_ref,
                h_ref, *rest, tm, emit_v):
    if emit_v:
        v_ref, gated_ref = rest
    else:
        (gated_ref,) = rest
    x = x_ref[0]
    hn = (_rms_hat(x) * g_ref[...]).astype(BF16)
    zu = _dot(hn, win_ref[:, :D_MODEL])
    zv = _dot(hn, win_ref[:, D_MODEL:])
    u = jax.nn.gelu(zu, approximate=True)
    v = jax.nn.gelu(zv, approximate=True)
    vc = v - jnp.mean(v, axis=-1, keepdims=True)
    v = vc * lax.rsqrt(jnp.mean(vc * vc, axis=-1, keepdims=True) + EPS)
    v = v * lng_ref[...] + lnb_ref[...]
    if emit_v:
        v_ref[0] = v
    vb = v.astype(BF16)
    bias = bs_ref[...]
    for r in range(tm // SGU_BLOCK):
        rows = slice(r * SGU_BLOCK, (r + 1) * SGU_BLOCK)
        for g in range(SGU_GROUPS):
            cols = slice(g * SGU_GROUP_DIM, (g + 1) * SGU_GROUP_DIM)
            mixed = _dot(ws_ref[g], vb[rows, cols]) + bias[:, cols]
            gated_ref[rows, cols] = (u[rows, cols] * mixed).astype(BF16)
    h_ref[0] = x + _dot(gated_ref[...], wout_ref[...])


def _sgu_mixer(x, g, w_in, ln_g, ln_b, ws, bs_full, w_out, *, tm, emit_v):
    nb, s, d = x.shape
    tok_spec = pl.BlockSpec((1, tm, d), lambda b, t: (b, t, 0))
    out_shape = [jax.ShapeDtypeStruct(x.shape, F32)]
    out_specs = [tok_spec]
    if emit_v:
        out_shape.append(jax.ShapeDtypeStruct(x.shape, F32))
        out_specs.append(tok_spec)
    res = pl.pallas_call(
        functools.partial(_sgu_kernel, tm=tm, emit_v=emit_v),
        grid=(nb, s // tm),
        in_specs=[tok_spec, _const_spec(g.shape), _const_spec(w_in.shape), _const_spec(ln_g.shape),
                  _const_spec(ln_b.shape), _const_spec(ws.shape), _const_spec(bs_full.shape),
                  _const_spec(w_out.shape)],
        out_specs=out_specs,
        out_shape=out_shape,
        scratch_shapes=[pltpu.VMEM((tm, d), BF16)],
        compiler_params=_params(("parallel", "parallel")),
        name="sgu_mixer",
    )(x, g, w_in, ln_g, ln_b, ws, bs_full, w_out)
    return res if emit_v else res[0]


def _ff_chunks():
    width, chunks, start = 512, [], 0
    while start < D_FF:
        w = min(width, D_FF - start)
        chunks.append((start, w))
        start += w
    return chunks


def _causal_conv3(a, prev, w, b):
    row = lax.broadcasted_iota(jnp.int32, a.shape, 0)
    a1 = jnp.where(row == 0, prev[1:2], pltpu.roll(a, 1, 0))
    a2 = jnp.where(row == 0, prev[0:1], jnp.where(row == 1, prev[1:2], pltpu.roll(a, 2, 0)))
    return a2 * w[0:1] + a1 * w[1:2] + a * w[2:3] + b


def _ffn_kernel(*refs, tm, nsub, with_oproj):
    if with_oproj:
        (x_ref, att_ref, wo_ref, g_ref, wup_ref, cw_ref, cb_ref, wdn_ref, st_ref,
         o_ref, ns_ref, carry_ref) = refs
    else:
        (x_ref, g_ref, wup_ref, cw_ref, cb_ref, wdn_ref, st_ref,
         o_ref, ns_ref, carry_ref) = refs
    seg = tm // nsub

    @pl.when(pl.program_id(1) == 0)
    def _():
        carry_ref[...] = st_ref[...]

    x = x_ref[0]
    if with_oproj:
        x = x + _dot(att_ref[0], wo_ref[...])
    hn = (_rms_hat(x) * g_ref[...]).astype(BF16)
    acc = x
    for start, width in _ff_chunks():
        halves = []
        for off in (start, D_FF + start):
            cols = slice(off, off + width)
            a = _dot(hn, wup_ref[:, cols])
            w = cw_ref[:, cols]
            b = cb_ref[:, cols]
            pieces = []
            for q in range(nsub):
                a_q = a[q * seg:(q + 1) * seg]
                pieces.append(_causal_conv3(a_q, carry_ref[q, :, cols], w, b))
                carry_ref[q, :, cols] = a_q[seg - (CONV_W - 1):]
            halves.append(pieces[0] if nsub == 1 else jnp.concatenate(pieces, axis=0))
        gate, val = halves
        act = (gate * jax.nn.sigmoid(gate) * val).astype(BF16)
        acc = acc + _dot(act, wdn_ref[start:start + width, :])
    o_ref[0] = acc
    ns_ref[...] = carry_ref[...]


def _conv_ffn(x, g, w_up, conv_w, conv_b, w_down, state, *, tm, nsub, att=None, w_o=None):
    nb, s, d = x.shape
    with_oproj = att is not None
    tok_spec = pl.BlockSpec((1, tm, d), lambda b, t: (b, t, 0))
    st_spec = pl.BlockSpec((nsub, CONV_W - 1, 2 * D_FF), lambda b, t: (b, 0, 0))
    operands = [x]
    in_specs = [tok_spec]
    if with_oproj:
        operands += [att, w_o]
        in_specs += [tok_spec, _const_spec(w_o.shape)]
    operands += [g, w_up, conv_w, conv_b, w_down, state]
    in_specs += [_const_spec(g.shape), _const_spec(w_up.shape), _const_spec(conv_w.shape),
                 _const_spec(conv_b.shape), _const_spec(w_down.shape), st_spec]
    return pl.pallas_call(
        functools.partial(_ffn_kernel, tm=tm, nsub=nsub, with_oproj=with_oproj),
        grid=(nb, s // tm),
        in_specs=in_specs,
        out_specs=[tok_spec, st_spec],
        out_shape=[jax.ShapeDtypeStruct(x.shape, F32), jax.ShapeDtypeStruct(state.shape, F32)],
        scratch_shapes=[pltpu.VMEM((nsub, CONV_W - 1, 2 * D_FF), F32)],
        compiler_params=_params(("parallel", "arbitrary")),
        name="conv_ffn",
    )(*operands)


def _head_rms(x, e_ref, et_ref, g):
    hi, lo = _split2(x * x)
    ss = _dot(hi, e_ref[...]) + _dot(lo, e_ref[...])
    r = lax.rsqrt(ss * (1.0 / HEAD_DIM) + EPS)
    rhi, rlo = _split2(r)
    rb = _dot(rhi, et_ref[...]) + _dot(rlo, et_ref[...])
    return x * rb * g


def _proj_kernel(x_ref, gkv_ref, gq_ref, wk_ref, wv_ref, wq_ref, wf_ref, bf_ref, kg_ref, qg_ref,
                 e_ref, et_ref, k_ref, v_ref, lf_ref, qb_ref, kb_ref, vb_ref):
    xh = _rms_hat(x_ref[0])
    hkv = (xh * gkv_ref[...]).astype(BF16)
    hq = (xh * gq_ref[...]).astype(BF16)
    k = _head_rms(_dot(hkv, wk_ref[...]), e_ref, et_ref, kg_ref[...])
    k_ref[0] = k
    kb_ref[0] = k.astype(BF16)
    v = _dot(hkv, wv_ref[...])
    v_ref[0] = v
    vb_ref[0] = v.astype(BF16)
    f = _dot(hkv, wf_ref[...]) + bf_ref[...]
    lf_ref[0] = jax.nn.log_sigmoid(f)[:, :N_HEADS]
    q = _head_rms(_dot(hq, wq_ref[...]), e_ref, et_ref, qg_ref[...])
    qb_ref[0] = (q * (HEAD_DIM ** -0.5)).astype(BF16)


def _kv_q_proj(x, g_kv, g_q, w_k, w_v, w_q, w_f, b_f, k_g, q_g, e, et, *, tm):
    nb, s, d = x.shape
    tok_spec = pl.BlockSpec((1, tm, d), lambda b, t: (b, t, 0))
    lf_spec = pl.BlockSpec((1, tm, N_HEADS), lambda b, t: (b, t, 0))
    consts = [g_kv, g_q, w_k, w_v, w_q, w_f, b_f, k_g, q_g, e, et]
    return pl.pallas_call(
        _proj_kernel,
        grid=(nb, s // tm),
        in_specs=[tok_spec] + [_const_spec(c.shape) for c in consts],
        out_specs=[tok_spec, tok_spec, lf_spec, tok_spec, tok_spec, tok_spec],
        out_shape=[jax.ShapeDtypeStruct(x.shape, F32), jax.ShapeDtypeStruct(x.shape, F32),
                   jax.ShapeDtypeStruct((nb, s, N_HEADS), F32),
                   jax.ShapeDtypeStruct(x.shape, BF16), jax.ShapeDtypeStruct(x.shape, BF16),
                   jax.ShapeDtypeStruct(x.shape, BF16)],
        compiler_params=_params(("parallel", "parallel")),
        name="kv_q_proj",
    )(x, *consts)


CUMSUM_TILE = 512


def _cumsum_kernel(x_ref, tri_ref, o_ref, carry_ref):
    @pl.when(pl.program_id(1) == 0)
    def _():
        carry_ref[...] = jnp.zeros_like(carry_ref)

    tri = tri_ref[...]
    out = carry_ref[...]
    for part in _split3(x_ref[0]):
        out = out + _dot(part, tri)
    o_ref[0] = out
    carry_ref[...] = jnp.broadcast_to(out[:, CUMSUM_TILE - 1:], out.shape)


def _cumsum_lanes(x_t):
    nb, h, s = x_t.shape
    assert s % CUMSUM_TILE == 0, s
    tri =jnp.asarray(np.triu(np.ones((CUMSUM_TILE, CUMSUM_TILE), np.float32)), BF16)
    spec = pl.BlockSpec((1, h, CUMSUM_TILE), lambda b, t: (b, 0, t))
    return pl.pallas_call(
        _cumsum_kernel,
        grid=(nb, s // CUMSUM_TILE),
        in_specs=[spec, _const_spec(tri.shape)],
        out_specs=spec,
        out_shape=jax.ShapeDtypeStruct(x_t.shape, F32),
        scratch_shapes=[pltpu.VMEM((h, CUMSUM_TILE), F32)],
        compiler_params=_params(("parallel", "arbitrary")),
        name="logf_cumsum",
    )(x_t, tri)


def _softmax_step(state, s, v):
    m, l, acc = state
    m_new = jnp.maximum(m, jnp.max(s, axis=-1, keepdims=True))
    alpha = jnp.exp(m - m_new)
    p = jnp.exp(s - m_new)
    l = alpha * l + jnp.sum(p, axis=-1, keepdims=True)
    acc = alpha * acc + _dot(p.astype(BF16), v)
    return m_new, l, acc


def _attn_kernel(q_ref, k_ref, v_ref, c_ref, o_ref, *, tq, tk):
    i = pl.program_id(2)
    q = q_ref[0]
    lane = lax.broadcasted_iota(jnp.int32, q.shape, 1)
    first = lane < HEAD_DIM
    qs = (jnp.where(first, q, jnp.zeros_like(q)), jnp.where(first, jnp.zeros_like(q), q))
    q0 = pl.multiple_of(i * tq, tq)
    c_row0 = c_ref[0, 0, :, pl.ds(q0, LANES)][:, 0:1]

    def chunk(k0, states, masked):
        k = k_ref[0, pl.ds(k0, tk), :]
        v = v_ref[0, pl.ds(k0, tk), :]
        ck = c_ref[0, 0, :, pl.ds(k0, tk)]
        out = []
        for hh in range(HEADS_PER_LANE_TILE):
            s = _dot_nt(qs[hh], k) + (c_row0[hh:hh + 1] - ck[hh:hh + 1])
            if masked is not None:
                s = jnp.where(masked, s, NEG_BIG)
            out.append(_softmax_step(states[hh], s, v))
        return tuple(out)

    init = tuple((jnp.full((tq, 1), NEG_BIG, F32), jnp.zeros((tq, 1), F32),
                  jnp.zeros((tq, LANES), F32)) for _ in range(HEADS_PER_LANE_TILE))
    n_full = i * (tq // tk)
    states = lax.fori_loop(
        0, n_full, lambda j, st: chunk(pl.multiple_of(j * tk, tk), st, None), init)
    row = lax.broadcasted_iota(jnp.int32, (tq, tk), 0)
    col = lax.broadcasted_iota(jnp.int32, (tq, tk), 1)
    for d in range(tq // tk):
        states = chunk(pl.multiple_of(q0 + d * tk, tk), states, col + d * tk <= row)
    outs = [acc / l for (_, l, acc) in states]
    o_ref[0] = jnp.where(first, outs[0], outs[1]).astype(o_ref.dtype)


def _prompt_attention(qb, kb, vb, c_pairs, *, tq, tk):
    nb, s, d = qb.shape
    q_spec = pl.BlockSpec((1, tq, LANES), lambda b, p, i: (b, i, p))
    kv_spec = pl.BlockSpec((1, s, LANES), lambda b, p, i: (b, 0, p))
    c_spec = pl.BlockSpec((1, 1, HEADS_PER_LANE_TILE, s), lambda b, p, i: (b, p, 0, 0))
    return pl.pallas_call(
        functools.partial(_attn_kernel, tq=tq, tk=tk),
        grid=(nb, N_HEAD_PAIRS, s // tq),
        in_specs=[q_spec, kv_spec, kv_spec, c_spec],
        out_specs=q_spec,
        out_shape=jax.ShapeDtypeStruct(qb.shape, BF16),
        compiler_params=_params(("parallel", "parallel", "parallel")),
        name="fox_attention_prompt",
    )(qb, kb, vb, c_pairs)


def _sample_attn_kernel(q_ref, ck_ref, cv_ref, kn_ref, vn_ref, c_ref, o_ref,
                        qst_ref, m_ref, l_ref, acc_ref, *, sq, tkv, past):
    j = pl.program_id(1)
    rows = N_HEADS * sq

    @pl.when(j == 0)
    def _():
        q = q_ref[0]
        head_of_lane = lax.broadcasted_iota(jnp.int32, q.shape, 1) // HEAD_DIM
        for h in range(N_HEADS):
            qst_ref[h * sq:(h + 1) * sq, :] = jnp.where(head_of_lane == h, q, jnp.zeros_like(q))
        m_ref[...] = jnp.full(m_ref.shape, NEG_BIG, F32)
        l_ref[...] = jnp.zeros(l_ref.shape, F32)
        acc_ref[...] = jnp.zeros(acc_ref.shape, F32)

    c_row0 = c_ref[0, :, pl.ds(past, LANES)][:, 0:1]

    def bias_rows(ck):
        b = c_row0 - ck
        return jnp.concatenate(
            [jnp.broadcast_to(b[h:h + 1], (sq, b.shape[1])) for h in range(N_HEADS)], axis=0)

    def update(s, v):
        m, l, acc = _softmax_step((m_ref[...], l_ref[...], acc_ref[...]), s, v)
        m_ref[...] = m
        l_ref[...] = l
        acc_ref[...] = acc

    k0 = pl.multiple_of(j * tkv, tkv)
    kc = ck_ref[0].astype(BF16)
    vc = cv_ref[0].astype(BF16)
    update(_dot_nt(qst_ref[...], kc) + bias_rows(c_ref[0, :, pl.ds(k0, tkv)]), vc)

    @pl.when(j == pl.num_programs(1) - 1)
    def _():
        kn = kn_ref[0]
        s = _dot_nt(qst_ref[...], kn) + bias_rows(c_ref[0, :, pl.ds(past, LANES)])
        t = lax.broadcasted_iota(jnp.int32, s.shape, 0) % sq
        col = lax.broadcasted_iota(jnp.int32, s.shape, 1)
        update(jnp.where(col <= t, s, NEG_BIG), vn_ref[0])
        o_all = acc_ref[...] / l_ref[...]
        head_of_lane = lax.broadcasted_iota(jnp.int32, (sq, D_MODEL), 1) // HEAD_DIM
        out = jnp.zeros((sq, D_MODEL), F32)
        for h in range(N_HEADS):
            out = jnp.where(head_of_lane == h, o_all[h * sq:(h + 1) * sq], out)
        o_ref[0] = out.astype(o_ref.dtype)


def _sample_attention(qb, cache_k, cache_v, kn_pad, vn_pad, c_t, *, tkv):
    nb, sq, d = qb.shape
    past = cache_k.shape[1]
    rows = N_HEADS * sq
    q_spec = pl.BlockSpec((1, sq, d), lambda b, j: (b, 0, 0))
    cache_spec = pl.BlockSpec((1, tkv, d), lambda b, j: (b, j, 0))
    new_spec = pl.BlockSpec((1, LANES, d), lambda b, j: (b, 0, 0))
    c_spec = pl.BlockSpec((1, N_HEADS, c_t.shape[2]), lambda b, j: (b, 0, 0))
    return pl.pallas_call(
        functools.partial(_sample_attn_kernel, sq=sq, tkv=tkv, past=past),
        grid=(nb, past // tkv),
        in_specs=[q_spec, cache_spec, cache_spec, new_spec, new_spec, c_spec],
        out_specs=q_spec,
        out_shape=jax.ShapeDtypeStruct(qb.shape, BF16),
        scratch_shapes=[pltpu.VMEM((rows, d), BF16), pltpu.VMEM((rows, 1), F32),
                        pltpu.VMEM((rows, 1), F32), pltpu.VMEM((rows, d), F32)],
        compiler_params=_params(("parallel", "arbitrary")),
        name="fox_attention_sample",
    )(qb, cache_k, cache_v, kn_pad, vn_pad, c_t)


def _row(v):
    return v.reshape(1, -1).astype(F32)


def _run_group(x, conv_in, past, wts, *, sgu_blk, tm_mix, tm_ffn, nsub, tm_proj):
    nb, s, d = x.shape
    mask = (np.arange(SGU_BLOCK)[None, :] // CHUNK) <= (np.arange(SGU_BLOCK)[:, None] // CHUNK)
    reps = SGU_BLOCK // sgu_blk
    m_blk = jnp.asarray(mask[:sgu_blk, :sgu_blk])
    ws_blk = jnp.where(m_blk[None], wts['a_w_s'][0][:, :sgu_blk, :sgu_blk], 0.0)
    eye = jnp.eye(reps, dtype=F32)
    ws = jnp.einsum('rs,gij->grisj', eye, ws_blk).reshape(SGU_GROUPS, SGU_BLOCK, SGU_BLOCK)
    bs = jnp.tile(wts['a_b_s'][0][:, :sgu_blk], (1, reps))
    bs_full = jnp.repeat(bs.T, SGU_GROUP_DIM, axis=1)
    mix = _sgu_mixer(x, _row(wts['norm_mix'][0]), wts['a_w_in'][0], _row(wts['a_ln_g'][0]),
                     _row(wts['a_ln_b'][0]), ws.astype(BF16), bs_full, wts['a_w_out'][0],
                     tm=tm_mix, emit_v=past is not None)
    if past is not None:
        h, sgu_v = mix
    else:
        h, sgu_v = mix, None

    def ffn(h, layer, att=None):
        return _conv_ffn(h, _row(wts['norm_ffn'][layer]), wts['f_w_up'][layer],
                         wts['f_conv_w'][layer], _row(wts['f_conv_b'][layer]),
                         wts['f_w_down'][layer], conv_in[layer], tm=tm_ffn, nsub=nsub,
                         att=att, w_o=None if att is None else wts['b_w_o'][0])

    h, conv0 = ffn(h, 0)

    k, v, lf, qb, kb, vb = _kv_q_proj(
        h, _row(wts['kv_norm']), _row(wts['norm_mix'][1]), wts['w_k'], wts['w_v'], wts['b_w_q'][0],
        wts['w_f_pad'], wts['b_f_pad'], _row(jnp.tile(wts['k_norm_g'], N_HEADS)),
        _row(jnp.tile(wts['q_norm_g'][0], N_HEADS)), wts['head_sum'], wts['head_bcast'], tm=tm_proj)
    return h, sgu_v, conv0, k, v, lf, qb, kb, vb, ffn


def kernel(x_prompt, x_sample, cache_k, cache_v, cache_logf, cache_ffn_conv, norm_mix, norm_ffn,
           a_w_in, a_ln_g, a_ln_b, a_w_s, a_b_s, a_w_out, f_w_up, f_conv_w, f_conv_b, f_w_down,
           kv_norm, w_k, w_v, k_norm_g, w_f, b_f, b_w_q, q_norm_g, b_w_o):
    batch, seq, d = x_prompt.shape
    dec_batch, dec_seq, _ = x_sample.shape
    past = cache_k.shape[1]
    head_of_col = np.arange(d) // HEAD_DIM
    head_sum = (head_of_col[:, None] == np.arange(LANES)[None, :]).astype(np.float32)
    wts = {
        'norm_mix': norm_mix, 'norm_ffn': norm_ffn,
        'a_w_in': a_w_in.astype(BF16), 'a_ln_g': a_ln_g, 'a_ln_b': a_ln_b, 'a_w_s': a_w_s,
        'a_b_s': a_b_s, 'a_w_out': a_w_out.astype(BF16),
        'f_w_up': f_w_up.astype(BF16), 'f_conv_w': f_conv_w, 'f_conv_b': f_conv_b,
        'f_w_down': f_w_down.astype(BF16),
        'kv_norm': kv_norm, 'w_k': w_k.astype(BF16), 'w_v': w_v.astype(BF16), 'k_norm_g': k_norm_g,
        'w_f_pad': jnp.pad(w_f, ((0, 0), (0, LANES - N_HEADS))).astype(BF16),
        'b_f_pad': jnp.pad(b_f, (0, LANES - N_HEADS)).reshape(1, LANES),
        'b_w_q': b_w_q.astype(BF16), 'q_norm_g': q_norm_g, 'b_w_o': b_w_o.astype(BF16),
        'head_sum': jnp.asarray(head_sum, BF16), 'head_bcast': jnp.asarray(head_sum.T, BF16),
    }

    zero_conv = jnp.zeros((DEPTH, batch, CONV_W - 1, 2 * D_FF), F32)
    h_p, _, conv_p0, k_p, v_p, lf_p, qb, kb, vb, ffn_p = _run_group(
        x_prompt, zero_conv, None, wts, sgu_blk=SGU_BLOCK, tm_mix=512, tm_ffn=512, nsub=1,
        tm_proj=512)
    c_p = _cumsum_lanes(lf_p.transpose(0, 2, 1))
    att_p = _prompt_attention(
        qb, kb, vb, c_p.reshape(batch, N_HEAD_PAIRS, HEADS_PER_LANE_TILE, seq), tq=512, tk=512)
    y_p, conv_p1 = ffn_p(h_p, 1, att_p)

    rows = dec_batch * dec_seq
    xs = x_sample.reshape(1, rows, d)
    h_s, sgu_v, conv_s0, k_s, v_s, lf_s, qb, kb, vb, ffn_s = _run_group(
        xs, cache_ffn_conv, (cache_k, cache_v, cache_logf), wts, sgu_blk=min(SGU_BLOCK, dec_seq),
        tm_mix=rows, tm_ffn=rows, nsub=dec_batch, tm_proj=rows)
    lf_s = lf_s.reshape(dec_batch, dec_seq, N_HEADS)
    pad = LANES - dec_seq
    lf_all = jnp.concatenate(
        [cache_logf.astype(F32), lf_s,
         jnp.zeros((dec_batch, CUMSUM_TILE - dec_seq, N_HEADS), F32)], axis=1)
    c_s = _cumsum_lanes(lf_all.transpose(0, 2, 1))
    pad_new = lambda a: jnp.pad(a.reshape(dec_batch, dec_seq, d), ((0, 0), (0, pad), (0, 0)))
    att_s = _sample_attention(
        qb.reshape(dec_batch, dec_seq, d), cache_k.reshape(dec_batch, past, d),
        cache_v.reshape(dec_batch, past, d), pad_new(kb), pad_new(vb), c_s, tkv=1024)
    y_s, conv_s1 = ffn_s(h_s, 1, att_s.reshape(1, rows, d))

    heads = lambda a, b, s: a.reshape(b, s, N_HEADS, HEAD_DIM)
    return (y_p, y_s.reshape(dec_batch, dec_seq, d),
            sgu_v.reshape(N_A, dec_batch, dec_seq, d),
            jnp.stack([conv_p0, conv_p1]), jnp.stack([conv_s0, conv_s1]),
            heads(k_p, batch, seq), heads(v_p, batch, seq), lf_p,
            heads(k_s, dec_batch, dec_seq), heads(v_s, dec_batch, dec_seq), lf_s)
```

```python
import functools

import jax
import jax.numpy as jnp
import numpy as np
from jax import lax
from jax.experimental import pallas as pl
from jax.experimental.pallas import tpu as pltpu

D_MODEL = 1024
DEPTH = 2
N_A = DEPTH // 2
CHUNK = 64
SGU_BLOCK = 128
SGU_GROUPS = 4
SGU_GROUP_DIM = D_MODEL // SGU_GROUPS
D_FF = 2816
CONV_W = 3
N_HEADS = 16
HEAD_DIM = D_MODEL // N_HEADS
EPS = 1e-6

LANES = 128
SUBLANES = 8
HEADS_PER_LANE_TILE = LANES // HEAD_DIM
N_HEAD_PAIRS = N_HEADS // HEADS_PER_LANE_TILE
NEG_BIG = -1e30
LOG2E = 1.4426950408889634
N_BIAS_TERMS = 3
CHUNKS_PER_TRIP = 2
SGU_ROW_PARTS = 2
VMEM_LIMIT = 56 * 1024 * 1024

F32 = jnp.float32
BF16 = jnp.bfloat16


def _const_spec(shape):
    nd = len(shape)
    return pl.BlockSpec(shape, lambda *_: (0,) * nd, pipeline_mode=pl.Buffered(1))


def _params(semantics):
    return pltpu.CompilerParams(dimension_semantics=semantics, vmem_limit_bytes=VMEM_LIMIT)


def _rms_hat(x):
    return x * lax.rsqrt(jnp.mean(x * x, axis=-1, keepdims=True) + EPS)


def _split2(x):
    hi = x.astype(BF16)
    lo = (x - hi.astype(F32)).astype(BF16)
    return hi, lo


def _split3(x):
    hi = x.astype(BF16)
    r = x - hi.astype(F32)
    mid = r.astype(BF16)
    lo = (r - mid.astype(F32)).astype(BF16)
    return hi, mid, lo


def _dot(a, b):
    return jnp.dot(a, b, preferred_element_type=F32)


def _dot_nt(a, b):
    return lax.dot_general(a, b, (((1,), (1,)), ((), ())), preferred_element_type=F32)


def _sgu_kernel(x_ref, g_ref, win_ref, lng_ref, lnb_ref, ws_ref, bs_ref, wout_ref,
                h_ref, *rest, tm, emit_v):
    if emit_v:
        v_ref, gated_ref = rest
    else:
        (gated_ref,) = rest
    part = tm // SGU_ROW_PARTS
    assert part % SGU_BLOCK == 0
    bias = bs_ref[...]
    staged = []
    for pt in range(SGU_ROW_PARTS):
        x = x_ref[0, pt * part:(pt + 1) * part, :]
        hn = (_rms_hat(x) * g_ref[...]).astype(BF16)
        staged.append((x, _dot(hn, win_ref[:, :D_MODEL]), _dot(hn, win_ref[:, D_MODEL:])))
    for pt, (x, zu, zv) in enumerate(staged):
        u = jax.nn.gelu(zu, approximate=True)
        v = jax.nn.gelu(zv, approximate=True)
        vc = v - jnp.mean(v, axis=-1, keepdims=True)
        v = vc * lax.rsqrt(jnp.mean(vc * vc, axis=-1, keepdims=True) + EPS)
        v = v * lng_ref[...] + lnb_ref[...]
        if emit_v:
            v_ref[0, pt * part:(pt + 1) * part, :] = v
        vb = v.astype(BF16)
        for r in range(part // SGU_BLOCK):
            rows = slice(r * SGU_BLOCK, (r + 1) * SGU_BLOCK)
            out_rows = slice(pt * part + r * SGU_BLOCK, pt * part + (r + 1) * SGU_BLOCK)
            for g in range(SGU_GROUPS):
                cols = slice(g * SGU_GROUP_DIM, (g + 1) * SGU_GROUP_DIM)
                mixed = _dot(ws_ref[g], vb[rows, cols]) + bias[:, cols]
                gated_ref[out_rows, cols] = (u[rows, cols] * mixed).astype(BF16)
        part_rows = slice(pt * part, (pt + 1) * part)
        h_ref[0, part_rows, :] = x + _dot(gated_ref[part_rows, :], wout_ref[...])


def _sgu_mixer(x, g, w_in, ln_g, ln_b, ws, bs_full, w_out, *, tm, emit_v):
    nb, s, d = x.shape
    tok_spec = pl.BlockSpec((1, tm, d), lambda b, t: (b, t, 0))
    out_shape = [jax.ShapeDtypeStruct(x.shape, F32)]
    out_specs = [tok_spec]
    if emit_v:
        out_shape.append(jax.ShapeDtypeStruct(x.shape, F32))
        out_specs.append(tok_spec)
    res = pl.pallas_call(
        functools.partial(_sgu_kernel, tm=tm, emit_v=emit_v),
        grid=(nb, s // tm),
        in_specs=[tok_spec, _const_spec(g.shape), _const_spec(w_in.shape), _const_spec(ln_g.shape),
                  _const_spec(ln_b.shape), _const_spec(ws.shape), _const_spec(bs_full.shape),
                  _const_spec(w_out.shape)],
        out_specs=out_specs,
        out_shape=out_shape,
        scratch_shapes=[pltpu.VMEM((tm, d), BF16)],
        compiler_params=_params(("parallel", "parallel")),
        name="sgu_mixer",
    )(x, g, w_in, ln_g, ln_b, ws, bs_full, w_out)
    return res if emit_v else res[0]


def _ff_chunks():
    width, chunks, start = 512, [], 0
    while start < D_FF:
        w = min(width, D_FF - start)
        chunks.append((start, w))
        start += w
    return chunks


def _causal_conv3(a, prev, w, b):
    row = lax.broadcasted_iota(jnp.int32, (SUBLANES, a.shape[1]), 0)

    def shifted(k):
        r = pltpu.roll(a, k, 0)
        head = r[:SUBLANES]
        for j in range(k):
            head = jnp.where(row == j, prev[CONV_W - 1 - k + j:CONV_W - k + j], head)
        return jnp.concatenate([head, r[SUBLANES:]], axis=0)

    return shifted(2) * w[0:1] + shifted(1) * w[1:2] + a * w[2:3] + b


def _ffn_kernel(*refs, tm, nsub, with_oproj):
    if with_oproj:
        (x_ref, att_ref, wo_ref, g_ref, wup_ref, cw_ref, cb_ref, wdn_ref, st_ref,
         o_ref, ns_ref, carry_ref) = refs
    else:
        (x_ref, g_ref, wup_ref, cw_ref, cb_ref, wdn_ref, st_ref,
         o_ref, ns_ref, carry_ref) = refs
    seg = tm // nsub

    @pl.when(pl.program_id(1) == 0)
    def _():
        carry_ref[...] = st_ref[...]

    x = x_ref[0]
    if with_oproj:
        x = x + _dot(att_ref[0], wo_ref[...])
    hn = (_rms_hat(x) * g_ref[...]).astype(BF16)
    acc = x
    chunks = _ff_chunks()

    def up_proj(ci):
        start, width = chunks[ci]
        return [_dot(hn, wup_ref[:, off:off + width]) for off in (start, D_FF + start)]

    a_next = up_proj(0)
    for ci, (start, width) in enumerate(chunks):
        a_cur = a_next
        if ci + 1 < len(chunks):
            a_next = up_proj(ci + 1)
        halves = []
        for a, off in zip(a_cur, (start, D_FF + start)):
            cols = slice(off, off + width)
            w = cw_ref[:, cols]
            b = cb_ref[:, cols]
            pieces = []
            for q in range(nsub):
                a_q = a[q * seg:(q + 1) * seg]
                pieces.append(_causal_conv3(a_q, carry_ref[q, :, cols], w, b))
                carry_ref[q, :, cols] = a_q[seg - (CONV_W - 1):]
            halves.append(pieces[0] if nsub == 1 else jnp.concatenate(pieces, axis=0))
        gate, val = halves
        act = (gate * jax.nn.sigmoid(gate) * val).astype(BF16)
        acc = acc + _dot(act, wdn_ref[start:start + width, :])
    o_ref[0] = acc
    ns_ref[...] = carry_ref[...]


def _conv_ffn(x, g, w_up, conv_w, conv_b, w_down, state, *, tm, nsub, att=None, w_o=None):
    nb, s, d = x.shape
    with_oproj = att is not None
    tok_spec = pl.BlockSpec((1, tm, d), lambda b, t: (b, t, 0))
    st_spec = pl.BlockSpec((nsub, CONV_W - 1, 2 * D_FF), lambda b, t: (b, 0, 0))
    operands = [x]
    in_specs = [tok_spec]
    if with_oproj:
        operands += [att, w_o]
        in_specs += [tok_spec, _const_spec(w_o.shape)]
    operands += [g, w_up, conv_w, conv_b, w_down, state]
    in_specs += [_const_spec(g.shape), _const_spec(w_up.shape), _const_spec(conv_w.shape),
                 _const_spec(conv_b.shape), _const_spec(w_down.shape), st_spec]
    return pl.pallas_call(
        functools.partial(_ffn_kernel, tm=tm, nsub=nsub, with_oproj=with_oproj),
        grid=(nb, s // tm),
        in_specs=in_specs,
        out_specs=[tok_spec, st_spec],
        out_shape=[jax.ShapeDtypeStruct(x.shape, F32), jax.ShapeDtypeStruct(state.shape, F32)],
        scratch_shapes=[pltpu.VMEM((nsub, CONV_W - 1, 2 * D_FF), F32)],
        compiler_params=_params(("parallel", "arbitrary")),
        name="conv_ffn",
    )(*operands)


def _head_rms(x, e_ref, et_ref, g):
    hi, lo = _split2(x * x)
    ss = _dot(hi, e_ref[...]) + _dot(lo, e_ref[...])
    r = lax.rsqrt(ss * (1.0 / HEAD_DIM) + EPS)
    rhi, rlo = _split2(r)
    rb = _dot(rhi, et_ref[...]) + _dot(rlo, et_ref[...])
    return x * rb * g


def _proj_kernel(x_ref, gkv_ref, gq_ref, wk_ref, wv_ref, wq_ref, wf_ref, bf_ref, kg_ref, qg_ref,
                 e_ref, et_ref, k_ref, v_ref, lf_ref, qb_ref, kb_ref, vb_ref):
    xh = _rms_hat(x_ref[0])
    hkv = (xh * gkv_ref[...]).astype(BF16)
    hq = (xh * gq_ref[...]).astype(BF16)
    k_raw = _dot(hkv, wk_ref[...])
    v = _dot(hkv, wv_ref[...])
    f = _dot(hkv, wf_ref[...]) + bf_ref[...]
    q_raw = _dot(hq, wq_ref[...])
    k = _head_rms(k_raw, e_ref, et_ref, kg_ref[...])
    k_ref[0] = k
    kb_ref[0] = k.astype(BF16)
    v_ref[0] = v
    vb_ref[0] = v.astype(BF16)
    lf_ref[0] = jax.nn.log_sigmoid(f)[:, :N_HEADS]
    q = _head_rms(q_raw, e_ref, et_ref, qg_ref[...])
    qb_ref[0] = (q * (LOG2E * HEAD_DIM ** -0.5)).astype(BF16)


def _kv_q_proj(x, g_kv, g_q, w_k, w_v, w_q, w_f, b_f, k_g, q_g, e, et, *, tm):
    nb, s, d = x.shape
    tok_spec = pl.BlockSpec((1, tm, d), lambda b, t: (b, t, 0))
    lf_spec = pl.BlockSpec((1, tm, N_HEADS), lambda b, t: (b, t, 0))
    consts = [g_kv, g_q, w_k, w_v, w_q, w_f, b_f, k_g, q_g, e, et]
    return pl.pallas_call(
        _proj_kernel,
        grid=(nb, s // tm),
        in_specs=[tok_spec] + [_const_spec(c.shape) for c in consts],
        out_specs=[tok_spec, tok_spec, lf_spec, tok_spec, tok_spec, tok_spec],
        out_shape=[jax.ShapeDtypeStruct(x.shape, F32), jax.ShapeDtypeStruct(x.shape, F32),
                   jax.ShapeDtypeStruct((nb, s, N_HEADS), F32),
                   jax.ShapeDtypeStruct(x.shape, BF16), jax.ShapeDtypeStruct(x.shape, BF16),
                   jax.ShapeDtypeStruct(x.shape, BF16)],
        compiler_params=_params(("parallel", "parallel")),
        name="kv_q_proj",
    )(x, *consts)


CUMSUM_TILE = 512


def _cumsum_kernel(x_ref, tri_ref, o_ref, *rest, emit_pieces):
    carry_ref = rest[-1]

    @pl.when(pl.program_id(1) == 0)
    def _():
        carry_ref[...] = jnp.zeros_like(carry_ref)

    tri = tri_ref[...]
    out = carry_ref[...]
    for part in _split3(x_ref[0]):
        out = out + _dot(part, tri)
    o_ref[0] = out
    carry_ref[...] = jnp.broadcast_to(out[:, CUMSUM_TILE - 1:], out.shape)
    if emit_pieces:
        for piece_ref, piece in zip(rest[:N_BIAS_TERMS], _split3(out * (-LOG2E))):
            piece_ref[0] = piece


def _cumsum_lanes(x_t, *, emit_pieces):
    nb, h, s = x_t.shape
    assert s % CUMSUM_TILE == 0, s
    tri = jnp.asarray(np.triu(np.ones((CUMSUM_TILE, CUMSUM_TILE), np.float32)), BF16)
    spec = pl.BlockSpec((1, h, CUMSUM_TILE), lambda b, t: (b, 0, t))
    n_pieces = N_BIAS_TERMS if emit_pieces else 0
    return pl.pallas_call(
        functools.partial(_cumsum_kernel, emit_pieces=emit_pieces),
        grid=(nb, s // CUMSUM_TILE),
        in_specs=[spec, _const_spec(tri.shape)],
        out_specs=[spec] * (1 + n_pieces),
        out_shape=[jax.ShapeDtypeStruct(x_t.shape, F32)]
        + [jax.ShapeDtypeStruct(x_t.shape, BF16)] * n_pieces,
        scratch_shapes=[pltpu.VMEM((h, CUMSUM_TILE), F32)],
        compiler_params=_params(("parallel", "arbitrary")),
        name="logf_cumsum",
    )(x_t, tri)


def _softmax_step(state, s, v):
    m, l, acc = state
    m_new = jnp.maximum(m, jnp.max(s, axis=-1, keepdims=True))
    alpha = jnp.exp2(m - m_new)
    p = jnp.exp2(s - m_new)
    l = alpha * l + jnp.sum(p, axis=-1, keepdims=True)
    acc = alpha * acc + _dot(p.astype(BF16), v)
    return m_new, l, acc


def _attn_kernel(q_ref, k_ref, v_ref, cp_ref, o_ref, *, tq, tk):
    i = pl.program_id(2)
    q = q_ref[0]
    lane_q = lax.broadcasted_iota(jnp.int32, q.shape, 1)
    first_q = lane_q < HEAD_DIM
    ones_hi = jnp.where((lane_q >= HEAD_DIM) & (lane_q < HEAD_DIM + N_BIAS_TERMS), 1.0, 0.0)
    ones_hi = ones_hi.astype(BF16)
    ones_lo = jnp.where(lane_q < N_BIAS_TERMS, 1.0, 0.0).astype(BF16)
    qs = (jnp.where(first_q, q, ones_hi), jnp.where(first_q, ones_lo, q))
    first_k = lax.broadcasted_iota(jnp.int32, (tk, LANES), 1) < HEAD_DIM
    q0 = pl.multiple_of(i * tq, tq)

    def logits(k0, masked):
        k = k_ref[0, pl.ds(k0, tk), :]
        cp = cp_ref[0, pl.ds(k0, tk), :]
        ks = (jnp.where(first_k, k, cp), jnp.where(first_k, cp, k))
        ss = [_dot_nt(qs[hh], ks[hh]) for hh in range(HEADS_PER_LANE_TILE)]
        if masked is not None:
            ss = [jnp.where(masked, s, NEG_BIG) for s in ss]
        return ss

    def update(states, ss, k0):
        v = v_ref[0, pl.ds(k0, tk), :]
        one_v = jnp.ones_like(v)
        vs = (jnp.where(first_k, v, one_v), jnp.where(first_k, one_v, v))
        out = []
        for hh in range(HEADS_PER_LANE_TILE):
            m, acc = states[hh]
            m_new = jnp.maximum(m, jnp.max(ss[hh], axis=-1, keepdims=True))
            p = jnp.exp2(ss[hh] - m_new).astype(BF16)
            acc = jnp.exp2(m - m_new) * acc + _dot(p, vs[hh])
            out.append((m_new, acc))
        return tuple(out)

    def chunk_group(states, starts, masks):
        sss = [logits(k0, mk) for k0, mk in zip(starts, masks)]
        for ss, k0 in zip(sss, starts):
            states = update(states, ss, k0)
        return states

    per_block = tq // tk
    assert per_block % CHUNKS_PER_TRIP == 0
    init = tuple((jnp.full((tq, 1), NEG_BIG, F32), jnp.zeros((tq, LANES), F32))
                 for _ in range(HEADS_PER_LANE_TILE))

    def full_trip(jj, st):
        starts = [pl.multiple_of((jj * CHUNKS_PER_TRIP + u) * tk, tk) for u in range(CHUNKS_PER_TRIP)]
        return chunk_group(st, starts, [None] * CHUNKS_PER_TRIP)

    states = lax.fori_loop(0, i * (per_block // CHUNKS_PER_TRIP), full_trip, init)
    row = lax.broadcasted_iota(jnp.int32, (tq, tk), 0)
    col = lax.broadcasted_iota(jnp.int32, (tq, tk), 1)
    for d0 in range(0, per_block, CHUNKS_PER_TRIP):
        ds = range(d0, d0 + CHUNKS_PER_TRIP)
        states = chunk_group(states, [pl.multiple_of(q0 + d * tk, tk) for d in ds],
                             [col + d * tk <= row for d in ds])
    outs = [acc / pltpu.roll(acc, HEAD_DIM, 1) for (_, acc) in states]
    o_ref[0] = jnp.where(first_q, outs[0], outs[1]).astype(o_ref.dtype)


def _prompt_attention(qb, kb, vb, cp, *, tq, tk):
    nb, s, d = qb.shape
    q_spec = pl.BlockSpec((1, tq, LANES), lambda b, p, i: (b, i, p))
    kv_spec = pl.BlockSpec((1, s, LANES), lambda b, p, i: (b, 0, p))
    return pl.pallas_call(
        functools.partial(_attn_kernel, tq=tq, tk=tk),
        grid=(nb, N_HEAD_PAIRS, s // tq),
        in_specs=[q_spec, kv_spec, kv_spec, kv_spec],
        out_specs=q_spec,
        out_shape=jax.ShapeDtypeStruct(qb.shape, BF16),
        compiler_params=_params(("parallel", "parallel", "parallel")),
        name="fox_attention_prompt",
    )(qb, kb, vb, cp)


def _bias_tiles(pieces):
    nb, h, s = pieces[0].shape
    x = jnp.stack(pieces, axis=-1).transpose(0, 2, 1, 3)
    x = x.reshape(nb, s, N_HEAD_PAIRS, HEADS_PER_LANE_TILE, N_BIAS_TERMS)
    fill = jnp.zeros((nb, s, N_HEAD_PAIRS, HEAD_DIM - N_BIAS_TERMS), BF16)
    tile = jnp.concatenate([x[:, :, :, 1], fill, x[:, :, :, 0], fill], axis=-1)
    return tile.reshape(nb, s, h * HEAD_DIM)


def _sample_attn_kernel(q_ref, ck_ref, cv_ref, kn_ref, vn_ref, c_ref, o_ref,
                        qst_ref, m_ref, l_ref, acc_ref, *, sq, tkv, past):
    j = pl.program_id(1)
    rows = N_HEADS * sq

    @pl.when(j == 0)
    def _():
        q = q_ref[0]
        head_of_lane = lax.broadcasted_iota(jnp.int32, q.shape, 1) // HEAD_DIM
        for h in range(N_HEADS):
            qst_ref[h * sq:(h + 1) * sq, :] = jnp.where(head_of_lane == h, q, jnp.zeros_like(q))
        m_ref[...] = jnp.full(m_ref.shape, NEG_BIG, F32)
        l_ref[...] = jnp.zeros(l_ref.shape, F32)
        acc_ref[...] = jnp.zeros(acc_ref.shape, F32)

    c_row0 = c_ref[0, :, pl.ds(past, LANES)][:, 0:1]

    def bias_rows(ck):
        b = (c_row0 - ck) * LOG2E
        return jnp.concatenate(
            [jnp.broadcast_to(b[h:h + 1], (sq, b.shape[1])) for h in range(N_HEADS)], axis=0)

    def update(s, v):
        m, l, acc = _softmax_step((m_ref[...], l_ref[...], acc_ref[...]), s, v)
        m_ref[...] = m
        l_ref[...] = l
        acc_ref[...] = acc

    k0 = pl.multiple_of(j * tkv, tkv)
    kc = ck_ref[0].astype(BF16)
    vc = cv_ref[0].astype(BF16)
    update(_dot_nt(qst_ref[...], kc) + bias_rows(c_ref[0, :, pl.ds(k0, tkv)]), vc)

    @pl.when(j == pl.num_programs(1) - 1)
    def _():
        kn = kn_ref[0]
        s = _dot_nt(qst_ref[...], kn) + bias_rows(c_ref[0, :, pl.ds(past, LANES)])
        t = lax.broadcasted_iota(jnp.int32, s.shape, 0) % sq
        col = lax.broadcasted_iota(jnp.int32, s.shape, 1)
        update(jnp.where(col <= t, s, NEG_BIG), vn_ref[0])
        o_all = acc_ref[...] / l_ref[...]
        head_of_lane = lax.broadcasted_iota(jnp.int32, (sq, D_MODEL), 1) // HEAD_DIM
        out = jnp.zeros((sq, D_MODEL), F32)
        for h in range(N_HEADS):
            out = jnp.where(head_of_lane == h, o_all[h * sq:(h + 1) * sq], out)
        o_ref[0] = out.astype(o_ref.dtype)


def _sample_attention(qb, cache_k, cache_v, kn_pad, vn_pad, c_t, *, tkv):
    nb, sq, d = qb.shape
    past = cache_k.shape[1]
    rows = N_HEADS * sq
    q_spec = pl.BlockSpec((1, sq, d), lambda b, j: (b, 0, 0))
    cache_spec = pl.BlockSpec((1, tkv, d), lambda b, j: (b, j, 0))
    new_spec = pl.BlockSpec((1, LANES, d), lambda b, j: (b, 0, 0))
    c_spec = pl.BlockSpec((1, N_HEADS, c_t.shape[2]), lambda b, j: (b, 0, 0))
    return pl.pallas_call(
        functools.partial(_sample_attn_kernel, sq=sq, tkv=tkv, past=past),
        grid=(nb, past // tkv),
        in_specs=[q_spec, cache_spec, cache_spec, new_spec, new_spec, c_spec],
        out_specs=q_spec,
        out_shape=jax.ShapeDtypeStruct(qb.shape, BF16),
        scratch_shapes=[pltpu.VMEM((rows, d), BF16), pltpu.VMEM((rows, 1), F32),
                        pltpu.VMEM((rows, 1), F32), pltpu.VMEM((rows, d), F32)],
        compiler_params=_params(("parallel", "arbitrary")),
        name="fox_attention_sample",
    )(qb, cache_k, cache_v, kn_pad, vn_pad, c_t)


def _row(v):
    return v.reshape(1, -1).astype(F32)


def _run_group(x, conv_in, past, wts, *, sgu_blk, tm_mix, tm_ffn, nsub, tm_proj):
    nb, s, d = x.shape
    mask = (np.arange(SGU_BLOCK)[None, :] // CHUNK) <= (np.arange(SGU_BLOCK)[:, None] // CHUNK)
    reps = SGU_BLOCK // sgu_blk
    m_blk = jnp.asarray(mask[:sgu_blk, :sgu_blk])
    ws_blk = jnp.where(m_blk[None], wts['a_w_s'][0][:, :sgu_blk, :sgu_blk], 0.0)
    eye = jnp.eye(reps, dtype=F32)
    ws = jnp.einsum('rs,gij->grisj', eye, ws_blk).reshape(SGU_GROUPS, SGU_BLOCK, SGU_BLOCK)
    bs = jnp.tile(wts['a_b_s'][0][:, :sgu_blk], (1, reps))
    bs_full = jnp.repeat(bs.T, SGU_GROUP_DIM, axis=1)
    mix = _sgu_mixer(x, _row(wts['norm_mix'][0]), wts['a_w_in'][0], _row(wts['a_ln_g'][0]),
                     _row(wts['a_ln_b'][0]), ws.astype(BF16), bs_full, wts['a_w_out'][0],
                     tm=tm_mix, emit_v=past is not None)
    if past is not None:
        h, sgu_v = mix
    else:
        h, sgu_v = mix, None

    def ffn(h, layer, att=None):
        return _conv_ffn(h, _row(wts['norm_ffn'][layer]), wts['f_w_up'][layer],
                         wts['f_conv_w'][layer], _row(wts['f_conv_b'][layer]),
                         wts['f_w_down'][layer], conv_in[layer], tm=tm_ffn, nsub=nsub,
                         att=att, w_o=None if att is None else wts['b_w_o'][0])

    h, conv0 = ffn(h, 0)

    k, v, lf, qb, kb, vb = _kv_q_proj(
        h, _row(wts['kv_norm']), _row(wts['norm_mix'][1]), wts['w_k'], wts['w_v'], wts['b_w_q'][0],
        wts['w_f_pad'], wts['b_f_pad'], _row(jnp.tile(wts['k_norm_g'], N_HEADS)),
        _row(jnp.tile(wts['q_norm_g'][0], N_HEADS)), wts['head_sum'], wts['head_bcast'], tm=tm_proj)
    return h, sgu_v, conv0, k, v, lf, qb, kb, vb, ffn


def kernel(x_prompt, x_sample, cache_k, cache_v, cache_logf, cache_ffn_conv, norm_mix, norm_ffn,
           a_w_in, a_ln_g, a_ln_b, a_w_s, a_b_s, a_w_out, f_w_up, f_conv_w, f_conv_b, f_w_down,
           kv_norm, w_k, w_v, k_norm_g, w_f, b_f, b_w_q, q_norm_g, b_w_o):
    batch, seq, d = x_prompt.shape
    dec_batch, dec_seq, _ = x_sample.shape
    past = cache_k.shape[1]
    head_of_col = np.arange(d) // HEAD_DIM
    head_sum = (head_of_col[:, None] == np.arange(LANES)[None, :]).astype(np.float32)
    wts = {
        'norm_mix': norm_mix, 'norm_ffn': norm_ffn,
        'a_w_in': a_w_in.astype(BF16), 'a_ln_g': a_ln_g, 'a_ln_b': a_ln_b, 'a_w_s': a_w_s,
        'a_b_s': a_b_s, 'a_w_out': a_w_out.astype(BF16),
        'f_w_up': f_w_up.astype(BF16), 'f_conv_w': f_conv_w, 'f_conv_b': f_conv_b,
        'f_w_down': f_w_down.astype(BF16),
        'kv_norm': kv_norm, 'w_k': w_k.astype(BF16), 'w_v': w_v.astype(BF16), 'k_norm_g': k_norm_g,
        'w_f_pad': jnp.pad(w_f, ((0, 0), (0, LANES - N_HEADS))).astype(BF16),
        'b_f_pad': jnp.pad(b_f, (0, LANES - N_HEADS)).reshape(1, LANES),
        'b_w_q': b_w_q.astype(BF16), 'q_norm_g': q_norm_g, 'b_w_o': b_w_o.astype(BF16),
        'head_sum': jnp.asarray(head_sum, BF16), 'head_bcast': jnp.asarray(head_sum.T, BF16),
    }

    zero_conv = jnp.zeros((DEPTH, batch, CONV_W - 1, 2 * D_FF), F32)
    h_p, _, conv_p0, k_p, v_p, lf_p, qb, kb, vb, ffn_p = _run_group(
        x_prompt, zero_conv, None, wts, sgu_blk=SGU_BLOCK, tm_mix=1024, tm_ffn=1024, nsub=1,
        tm_proj=512)
    _, *bias_pieces = _cumsum_lanes(lf_p.transpose(0, 2, 1), emit_pieces=True)
    att_p = _prompt_attention(qb, kb, vb, _bias_tiles(bias_pieces), tq=1024, tk=512)
    y_p, conv_p1 = ffn_p(h_p, 1, att_p)

    rows = dec_batch * dec_seq
    xs = x_sample.reshape(1, rows, d)
    h_s, sgu_v, conv_s0, k_s, v_s, lf_s, qb, kb, vb, ffn_s = _run_group(
        xs, cache_ffn_conv, (cache_k, cache_v, cache_logf), wts, sgu_blk=min(SGU_BLOCK, dec_seq),
        tm_mix=rows, tm_ffn=rows, nsub=dec_batch, tm_proj=rows)
    lf_s = lf_s.reshape(dec_batch, dec_seq, N_HEADS)
    pad = LANES - dec_seq
    lf_all = jnp.concatenate(
        [cache_logf.astype(F32), lf_s,
         jnp.zeros((dec_batch, CUMSUM_TILE - dec_seq, N_HEADS), F32)], axis=1)
    (c_s,) = _cumsum_lanes(lf_all.transpose(0, 2, 1), emit_pieces=False)
    pad_new = lambda a: jnp.pad(a.reshape(dec_batch, dec_seq, d), ((0, 0), (0, pad), (0, 0)))
    att_s = _sample_attention(
        qb.reshape(dec_batch, dec_seq, d), cache_k.reshape(dec_batch, past, d),
        cache_v.reshape(dec_batch, past, d), pad_new(kb), pad_new(vb), c_s, tkv=1024)
    y_s, conv_s1 = ffn_s(h_s, 1, att_s.reshape(1, rows, d))

    heads = lambda a, b, s: a.reshape(b, s, N_HEADS, HEAD_DIM)
    return (y_p, y_s.reshape(dec_batch, dec_seq, d),
            sgu_v.reshape(N_A, dec_batch, dec_seq, d),
            jnp.stack([conv_p0, conv_p1]), jnp.stack([conv_s0, conv_s1]),
            heads(k_p, batch, seq), heads(v_p, batch, seq), lf_p,
            heads(k_s, dec_batch, dec_seq), heads(v_s, dec_batch, dec_seq), lf_s)
```

```python
import functools

import jax
import jax.numpy as jnp
import numpy as np
from jax import lax
from jax.experimental import pallas as pl
from jax.experimental.pallas import tpu as pltpu

D_MODEL = 1024
DEPTH = 2
N_A = DEPTH // 2
CHUNK = 64
SGU_BLOCK = 128
SGU_GROUPS = 4
SGU_GROUP_DIM = D_MODEL // SGU_GROUPS
D_FF = 2816
CONV_W = 3
N_HEADS = 16
HEAD_DIM = D_MODEL // N_HEADS
EPS = 1e-6

LANES = 128
SUBLANES = 8
HEADS_PER_LANE_TILE = LANES // HEAD_DIM
N_HEAD_PAIRS = N_HEADS // HEADS_PER_LANE_TILE
NEG_BIG = -1e30
LOG2E = 1.4426950408889634
N_BIAS_TERMS = 3
CHUNKS_PER_TRIP = 2
SGU_ROW_PARTS = 2
PROJ_ROW_PARTS = 2
VMEM_LIMIT = 56 * 1024 * 1024

F32 = jnp.float32
BF16 = jnp.bfloat16


def _const_spec(shape):
    nd = len(shape)
    return pl.BlockSpec(shape, lambda *_: (0,) * nd, pipeline_mode=pl.Buffered(1))


def _params(semantics):
    return pltpu.CompilerParams(dimension_semantics=semantics, vmem_limit_bytes=VMEM_LIMIT)


def _rms_hat(x):
    return x * lax.rsqrt(jnp.mean(x * x, axis=-1, keepdims=True) + EPS)


def _split2(x):
    hi = x.astype(BF16)
    lo = (x - hi.astype(F32)).astype(BF16)
    return hi, lo


def _split3(x):
    hi = x.astype(BF16)
    r = x - hi.astype(F32)
    mid = r.astype(BF16)
    lo = (r - mid.astype(F32)).astype(BF16)
    return hi, mid, lo


def _dot(a, b):
    return jnp.dot(a, b, preferred_element_type=F32)


def _dot_nt(a, b):
    return lax.dot_general(a, b, (((1,), (1,)), ((), ())), preferred_element_type=F32)


def _sgu_kernel(x_ref, g_ref, win_ref, lng_ref, lnb_ref, ws_ref, bs_ref, wout_ref,
                h_ref, *rest, tm, emit_v):
    if emit_v:
        v_ref, gated_ref = rest
    else:
        (gated_ref,) = rest
    part = tm // SGU_ROW_PARTS
    assert part % SGU_BLOCK == 0
    bias = bs_ref[...]
    staged = []
    for pt in range(SGU_ROW_PARTS):
        x = x_ref[0, pt * part:(pt + 1) * part, :]
        hn = (_rms_hat(x) * g_ref[...]).astype(BF16)
        staged.append((x, _dot(hn, win_ref[:, :D_MODEL]), _dot(hn, win_ref[:, D_MODEL:])))
    for pt, (x, zu, zv) in enumerate(staged):
        u = jax.nn.gelu(zu, approximate=True)
        v = jax.nn.gelu(zv, approximate=True)
        vc = v - jnp.mean(v, axis=-1, keepdims=True)
        v = vc * lax.rsqrt(jnp.mean(vc * vc, axis=-1, keepdims=True) + EPS)
        v = v * lng_ref[...] + lnb_ref[...]
        if emit_v:
            v_ref[0, pt * part:(pt + 1) * part, :] = v
        vb = v.astype(BF16)
        for r in range(part // SGU_BLOCK):
            rows = slice(r * SGU_BLOCK, (r + 1) * SGU_BLOCK)
            out_rows = slice(pt * part + r * SGU_BLOCK, pt * part + (r + 1) * SGU_BLOCK)
            for g in range(SGU_GROUPS):
                cols = slice(g * SGU_GROUP_DIM, (g + 1) * SGU_GROUP_DIM)
                mixed = _dot(ws_ref[g], vb[rows, cols]) + bias[:, cols]
                gated_ref[out_rows, cols] = (u[rows, cols] * mixed).astype(BF16)
        part_rows = slice(pt * part, (pt + 1) * part)
        h_ref[0, part_rows, :] = x + _dot(gated_ref[part_rows, :], wout_ref[...])


def _sgu_mixer(x, g, w_in, ln_g, ln_b, ws, bs_full, w_out, *, tm, emit_v):
    nb, s, d = x.shape
    tok_spec = pl.BlockSpec((1, tm, d), lambda b, t: (b, t, 0))
    out_shape = [jax.ShapeDtypeStruct(x.shape, F32)]
    out_specs = [tok_spec]
    if emit_v:
        out_shape.append(jax.ShapeDtypeStruct(x.shape, F32))
        out_specs.append(tok_spec)
    res = pl.pallas_call(
        functools.partial(_sgu_kernel, tm=tm, emit_v=emit_v),
        grid=(nb, s // tm),
        in_specs=[tok_spec, _const_spec(g.shape), _const_spec(w_in.shape), _const_spec(ln_g.shape),
                  _const_spec(ln_b.shape), _const_spec(ws.shape), _const_spec(bs_full.shape),
                  _const_spec(w_out.shape)],
        out_specs=out_specs,
        out_shape=out_shape,
        scratch_shapes=[pltpu.VMEM((tm, d), BF16)],
        compiler_params=_params(("parallel", "parallel")),
        name="sgu_mixer",
    )(x, g, w_in, ln_g, ln_b, ws, bs_full, w_out)
    return res if emit_v else res[0]


def _ff_chunks():
    width, chunks, start = 512, [], 0
    while start < D_FF:
        w = min(width, D_FF - start)
        chunks.append((start, w))
        start += w
    return chunks


def _causal_conv3(a, prev, w, b):
    row = lax.broadcasted_iota(jnp.int32, (SUBLANES, a.shape[1]), 0)

    def shifted(k):
        r = pltpu.roll(a, k, 0)
        head = r[:SUBLANES]
        for j in range(k):
            head = jnp.where(row == j, prev[CONV_W - 1 - k + j:CONV_W - k + j], head)
        return jnp.concatenate([head, r[SUBLANES:]], axis=0)

    return shifted(2) * w[0:1] + shifted(1) * w[1:2] + a * w[2:3] + b


def _ffn_kernel(*refs, tm, nsub, with_oproj):
    if with_oproj:
        (x_ref, att_ref, wo_ref, g_ref, wup_ref, cw_ref, cb_ref, wdn_ref, st_ref,
         o_ref, ns_ref, carry_ref) = refs
    else:
        (x_ref, g_ref, wup_ref, cw_ref, cb_ref, wdn_ref, st_ref,
         o_ref, ns_ref, carry_ref) = refs
    seg = tm // nsub

    @pl.when(pl.program_id(1) == 0)
    def _():
        carry_ref[...] = st_ref[...]

    x = x_ref[0]
    if with_oproj:
        x = x + _dot(att_ref[0], wo_ref[...])
    hn = (_rms_hat(x) * g_ref[...]).astype(BF16)
    acc = x
    chunks = _ff_chunks()

    def up_proj(ci):
        start, width = chunks[ci]
        return [_dot(hn, wup_ref[:, off:off + width]) for off in (start, D_FF + start)]

    a_next = up_proj(0)
    for ci, (start, width) in enumerate(chunks):
        a_cur = a_next
        if ci + 1 < len(chunks):
            a_next = up_proj(ci + 1)
        halves = []
        for a, off in zip(a_cur, (start, D_FF + start)):
            cols = slice(off, off + width)
            w = cw_ref[:, cols]
            b = cb_ref[:, cols]
            pieces = []
            for q in range(nsub):
                a_q = a[q * seg:(q + 1) * seg]
                pieces.append(_causal_conv3(a_q, carry_ref[q, :, cols], w, b))
                carry_ref[q, :, cols] = a_q[seg - (CONV_W - 1):]
            halves.append(pieces[0] if nsub == 1 else jnp.concatenate(pieces, axis=0))
        gate, val = halves
        act = (gate * jax.nn.sigmoid(gate) * val).astype(BF16)
        acc = acc + _dot(act, wdn_ref[start:start + width, :])
    o_ref[0] = acc
    ns_ref[...] = carry_ref[...]


def _conv_ffn(x, g, w_up, conv_w, conv_b, w_down, state, *, tm, nsub, att=None, w_o=None):
    nb, s, d = x.shape
    with_oproj = att is not None
    tok_spec = pl.BlockSpec((1, tm, d), lambda b, t: (b, t, 0))
    st_spec = pl.BlockSpec((nsub, CONV_W - 1, 2 * D_FF), lambda b, t: (b, 0, 0))
    operands = [x]
    in_specs = [tok_spec]
    if with_oproj:
        operands += [att, w_o]
        in_specs += [tok_spec, _const_spec(w_o.shape)]
    operands += [g, w_up, conv_w, conv_b, w_down, state]
    in_specs += [_const_spec(g.shape), _const_spec(w_up.shape), _const_spec(conv_w.shape),
                 _const_spec(conv_b.shape), _const_spec(w_down.shape), st_spec]
    return pl.pallas_call(
        functools.partial(_ffn_kernel, tm=tm, nsub=nsub, with_oproj=with_oproj),
        grid=(nb, s // tm),
        in_specs=in_specs,
        out_specs=[tok_spec, st_spec],
        out_shape=[jax.ShapeDtypeStruct(x.shape, F32), jax.ShapeDtypeStruct(state.shape, F32)],
        scratch_shapes=[pltpu.VMEM((nsub, CONV_W - 1, 2 * D_FF), F32)],
        compiler_params=_params(("parallel", "arbitrary")),
        name="conv_ffn",
    )(*operands)


def _head_rms(x, e_ref, et_ref, g):
    hi, lo = _split2(x * x)
    ss = _dot(hi, e_ref[...]) + _dot(lo, e_ref[...])
    r = lax.rsqrt(ss * (1.0 / HEAD_DIM) + EPS)
    rhi, rlo = _split2(r)
    rb = _dot(rhi, et_ref[...]) + _dot(rlo, et_ref[...])
    return x * rb * g


def _store_heads(ref, x, row0):
    n = x.shape[0]
    for h in range(N_HEADS):
        ref[0, pl.ds(row0 * N_HEADS + h, n, stride=N_HEADS), :] = x[:, h * HEAD_DIM:(h + 1) * HEAD_DIM]


def _proj_kernel(x_ref, gkv_ref, gq_ref, wk_ref, wv_ref, wq_ref, wf_ref, bf_ref, kg_ref, qg_ref,
                 e_ref, et_ref, *rest, tm, emit_bias):
    if emit_bias:
        (tril_ref, place_ref, k_ref, v_ref, lf_ref, qb_ref, kb_ref, vb_ref, cp_ref, carry_ref) = rest
    else:
        (k_ref, v_ref, lf_ref, qb_ref, kb_ref, vb_ref) = rest
    part = tm // PROJ_ROW_PARTS
    staged = []
    for pt in range(PROJ_ROW_PARTS):
        xh = _rms_hat(x_ref[0, pt * part:(pt + 1) * part, :])
        hkv = (xh * gkv_ref[...]).astype(BF16)
        hq = (xh * gq_ref[...]).astype(BF16)
        staged.append((_dot(hkv, wk_ref[...]), _dot(hkv, wv_ref[...]),
                       _dot(hkv, wf_ref[...]) + bf_ref[...],
                       _dot(hq, wq_ref[...])))
    if emit_bias:
        @pl.when(pl.program_id(1) == 0)
        def _():
            carry_ref[...] = jnp.zeros_like(carry_ref)

    for pt, (k_raw, v, f, q_raw) in enumerate(staged):
        rows = slice(pt * part, (pt + 1) * part)
        k = _head_rms(k_raw, e_ref, et_ref, kg_ref[...])
        _store_heads(k_ref, k, pt * part)
        kb_ref[0, rows, :] = k.astype(BF16)
        _store_heads(v_ref, v, pt * part)
        vb_ref[0, rows, :] = v.astype(BF16)
        lf = jax.nn.log_sigmoid(f)
        lf_ref[0, rows, :] = lf[:, :N_HEADS]
        q = _head_rms(q_raw, e_ref, et_ref, qg_ref[...])
        qb_ref[0, rows, :] = (q * (LOG2E * HEAD_DIM ** -0.5)).astype(BF16)
        if emit_bias:
            c = carry_ref[0:1, :]
            for piece in _split3(lf):
                c = c + _dot(tril_ref[...], piece)
            carry_ref[0:1, :] = c[part - 1:, :]
            cp = None
            for j, piece in enumerate(_split3(c * (-LOG2E))):
                term = _dot(piece, place_ref[j])
                cp = term if cp is None else cp + term
            cp_ref[0, rows, :] = cp.astype(BF16)


def _bias_placement():
    place = np.zeros((N_BIAS_TERMS, LANES, D_MODEL), np.float32)
    for h in range(N_HEADS):
        base = (h // HEADS_PER_LANE_TILE) * LANES + (HEAD_DIM if h % HEADS_PER_LANE_TILE == 0 else 0)
        for j in range(N_BIAS_TERMS):
            place[j, h, base + j] = 1.0
    return jnp.asarray(place, BF16)


def _kv_q_proj(x, g_kv, g_q, w_k, w_v, w_q, w_f, b_f, k_g, q_g, e, et, *, tm, emit_bias):
    nb, s, d = x.shape
    tok_spec = pl.BlockSpec((1, tm, d), lambda b, t: (b, t, 0))
    head_spec = pl.BlockSpec((1, tm * N_HEADS, HEAD_DIM), lambda b, t: (b, t, 0))
    lf_spec = pl.BlockSpec((1, tm, N_HEADS), lambda b, t: (b, t, 0))
    consts = [g_kv, g_q, w_k, w_v, w_q, w_f, b_f, k_g, q_g, e, et]
    out_specs = [head_spec, head_spec, lf_spec, tok_spec, tok_spec, tok_spec]
    heads_shape = jax.ShapeDtypeStruct((nb, s * N_HEADS, HEAD_DIM), F32)
    out_shape = [heads_shape, heads_shape, jax.ShapeDtypeStruct((nb, s, N_HEADS), F32),
                 jax.ShapeDtypeStruct(x.shape, BF16), jax.ShapeDtypeStruct(x.shape, BF16),
                 jax.ShapeDtypeStruct(x.shape, BF16)]
    scratch = []
    if emit_bias:
        part = tm // PROJ_ROW_PARTS
        consts += [jnp.asarray(np.tril(np.ones((part, part), np.float32)), BF16), _bias_placement()]
        out_specs.append(tok_spec)
        out_shape.append(jax.ShapeDtypeStruct(x.shape, BF16))
        scratch.append(pltpu.VMEM((SUBLANES, LANES), F32))
    return pl.pallas_call(
        functools.partial(_proj_kernel, tm=tm, emit_bias=emit_bias),
        grid=(nb, s // tm),
        in_specs=[tok_spec] + [_const_spec(c.shape) for c in consts],
        out_specs=out_specs,
        out_shape=out_shape,
        scratch_shapes=scratch,
        compiler_params=_params(("parallel", "arbitrary")),
        name="kv_q_proj",
    )(x, *consts)


CUMSUM_TILE = 512


def _cumsum_kernel(x_ref, tri_ref, o_ref, carry_ref):
    @pl.when(pl.program_id(0) == 0)
    def _():
        carry_ref[...] = jnp.zeros_like(carry_ref)

    tri = tri_ref[...]
    out = carry_ref[...]
    for part in _split3(x_ref[...]):
        out = out + _dot(part, tri)
    o_ref[...] = out
    carry_ref[...] = jnp.broadcast_to(out[:, CUMSUM_TILE - 1:], out.shape)


def _cumsum_lanes(x):
    r, s = x.shape
    assert s % CUMSUM_TILE == 0, s
    tri = jnp.asarray(np.triu(np.ones((CUMSUM_TILE, CUMSUM_TILE), np.float32)), BF16)
    spec = pl.BlockSpec((r, CUMSUM_TILE), lambda t: (0, t))
    return pl.pallas_call(
        _cumsum_kernel,
        grid=(s // CUMSUM_TILE,),
        in_specs=[spec, _const_spec(tri.shape)],
        out_specs=spec,
        out_shape=jax.ShapeDtypeStruct(x.shape, F32),
        scratch_shapes=[pltpu.VMEM((r, CUMSUM_TILE), F32)],
        compiler_params=_params(("arbitrary",)),
        name="logf_cumsum",
    )(x, tri)


def _softmax_step(state, s, v):
    m, l, acc = state
    m_new = jnp.maximum(m, jnp.max(s, axis=-1, keepdims=True))
    alpha = jnp.exp2(m - m_new)
    p = jnp.exp2(s - m_new)
    l = alpha * l + jnp.sum(p, axis=-1, keepdims=True)
    acc = alpha * acc + _dot(p.astype(BF16), v)
    return m_new, l, acc


def _attn_kernel(q_ref, k_ref, v_ref, cp_ref, o_ref, *, tq, tk):
    i = pl.program_id(2)
    q = q_ref[0]
    lane_q = lax.broadcasted_iota(jnp.int32, q.shape, 1)
    first_q = lane_q < HEAD_DIM
    ones_hi = jnp.where((lane_q >= HEAD_DIM) & (lane_q < HEAD_DIM + N_BIAS_TERMS), 1.0, 0.0)
    ones_hi = ones_hi.astype(BF16)
    ones_lo = jnp.where(lane_q < N_BIAS_TERMS, 1.0, 0.0).astype(BF16)
    qs = (jnp.where(first_q, q, ones_hi), jnp.where(first_q, ones_lo, q))
    first_k = lax.broadcasted_iota(jnp.int32, (tk, LANES), 1) < HEAD_DIM
    q0 = pl.multiple_of(i * tq, tq)

    def logits(k0, row0, masked):
        k = k_ref[0, pl.ds(k0, tk), :]
        cp = cp_ref[0, pl.ds(k0, tk), :]
        ks = (jnp.where(first_k, k, cp), jnp.where(first_k, cp, k))
        ss = [_dot_nt(qs[hh][row0:], ks[hh]) for hh in range(HEADS_PER_LANE_TILE)]
        if masked is not None:
            ss = [jnp.where(masked, s, NEG_BIG) for s in ss]
        return ss

    def update(states, ss, k0, row0):
        v = v_ref[0, pl.ds(k0, tk), :]
        one_v = jnp.ones_like(v)
        vs = (jnp.where(first_k, v, one_v), jnp.where(first_k, one_v, v))
        out = []
        for hh in range(HEADS_PER_LANE_TILE):
            m_all, acc_all = states[hh]
            m, acc = m_all[row0:], acc_all[row0:]
            m_new = jnp.maximum(m, jnp.max(ss[hh], axis=-1, keepdims=True))
            p = jnp.exp2(ss[hh] - m_new).astype(BF16)
            acc = jnp.exp2(m - m_new) * acc + _dot(p, vs[hh])
            if row0:
                m_new = jnp.concatenate([m_all[:row0], m_new], axis=0)
                acc = jnp.concatenate([acc_all[:row0], acc], axis=0)
            out.append((m_new, acc))
        return tuple(out)

    def chunk_group(states, starts, row0s, masks):
        sss = [logits(k0, r0, mk) for k0, r0, mk in zip(starts, row0s, masks)]
        for ss, k0, r0 in zip(sss, starts, row0s):
            states = update(states, ss, k0, r0)
        return states

    per_block = tq // tk
    assert per_block % CHUNKS_PER_TRIP == 0
    init = tuple((jnp.full((tq, 1), NEG_BIG, F32), jnp.zeros((tq, LANES), F32))
                 for _ in range(HEADS_PER_LANE_TILE))

    def full_trip(jj, st):
        starts = [pl.multiple_of((jj * CHUNKS_PER_TRIP + u) * tk, tk) for u in range(CHUNKS_PER_TRIP)]
        return chunk_group(st, starts, [0] * CHUNKS_PER_TRIP, [None] * CHUNKS_PER_TRIP)

    states = lax.fori_loop(0, i * (per_block // CHUNKS_PER_TRIP), full_trip, init)
    for d0 in range(0, per_block, CHUNKS_PER_TRIP):
        ds = range(d0, d0 + CHUNKS_PER_TRIP)
        masks = []
        for d in ds:
            row = lax.broadcasted_iota(jnp.int32, (tq - d * tk, tk), 0)
            col = lax.broadcasted_iota(jnp.int32, (tq - d * tk, tk), 1)
            masks.append(col <= row)
        states = chunk_group(states, [pl.multiple_of(q0 + d * tk, tk) for d in ds],
                             [d * tk for d in ds], masks)
    outs = [acc / pltpu.roll(acc, HEAD_DIM, 1) for (_, acc) in states]
    o_ref[0] = jnp.where(first_q, outs[0], outs[1]).astype(o_ref.dtype)


def _prompt_attention(qb, kb, vb, cp, *, tq, tk):
    nb, s, d = qb.shape
    q_spec = pl.BlockSpec((1, tq, LANES), lambda b, p, i: (b, i, p))
    kv_spec = pl.BlockSpec((1, s, LANES), lambda b, p, i: (b, 0, p))
    return pl.pallas_call(
        functools.partial(_attn_kernel, tq=tq, tk=tk),
        grid=(nb, N_HEAD_PAIRS, s // tq),
        in_specs=[q_spec, kv_spec, kv_spec, kv_spec],
        out_specs=q_spec,
        out_shape=jax.ShapeDtypeStruct(qb.shape, BF16),
        compiler_params=_params(("parallel", "parallel", "parallel")),
        name="fox_attention_prompt",
    )(qb, kb, vb, cp)


def _sample_attn_kernel(q_ref, ck_ref, cv_ref, kn_ref, vn_ref, c_ref, o_ref,
                        qst_ref, m_ref, l_ref, acc_ref, *, sq, tkv, past):
    j = pl.program_id(1)
    rows = N_HEADS * sq

    @pl.when(j == 0)
    def _():
        q = q_ref[0]
        head_of_lane = lax.broadcasted_iota(jnp.int32, q.shape, 1) // HEAD_DIM
        for h in range(N_HEADS):
            qst_ref[h * sq:(h + 1) * sq, :] = jnp.where(head_of_lane == h, q, jnp.zeros_like(q))
        m_ref[...] = jnp.full(m_ref.shape, NEG_BIG, F32)
        l_ref[...] = jnp.zeros(l_ref.shape, F32)
        acc_ref[...] = jnp.zeros(acc_ref.shape, F32)

    c_row0 = c_ref[0, :, pl.ds(past, LANES)][:, 0:1]

    def bias_rows(ck):
        b = (c_row0 - ck) * LOG2E
        return jnp.concatenate(
            [jnp.broadcast_to(b[h:h + 1], (sq, b.shape[1])) for h in range(N_HEADS)], axis=0)

    def update(s, v):
        m, l, acc = _softmax_step((m_ref[...], l_ref[...], acc_ref[...]), s, v)
        m_ref[...] = m
        l_ref[...] = l
        acc_ref[...] = acc

    k0 = pl.multiple_of(j * tkv, tkv)
    kc = ck_ref[0].astype(BF16)
    vc = cv_ref[0].astype(BF16)
    update(_dot_nt(qst_ref[...], kc) + bias_rows(c_ref[0, :, pl.ds(k0, tkv)]), vc)

    @pl.when(j == pl.num_programs(1) - 1)
    def _():
        kn = kn_ref[0]
        s = _dot_nt(qst_ref[...], kn) + bias_rows(c_ref[0, :, pl.ds(past, LANES)])
        t = lax.broadcasted_iota(jnp.int32, s.shape, 0) % sq
        col = lax.broadcasted_iota(jnp.int32, s.shape, 1)
        update(jnp.where(col <= t, s, NEG_BIG), vn_ref[0])
        o_all = acc_ref[...] / l_ref[...]
        head_of_lane = lax.broadcasted_iota(jnp.int32, (sq, D_MODEL), 1) // HEAD_DIM
        out = jnp.zeros((sq, D_MODEL), F32)
        for h in range(N_HEADS):
            out = jnp.where(head_of_lane == h, o_all[h * sq:(h + 1) * sq], out)
        o_ref[0] = out.astype(o_ref.dtype)


def _sample_attention(qb, cache_k, cache_v, kn_pad, vn_pad, c_t, *, tkv):
    nb, sq, d = qb.shape
    past = cache_k.shape[1]
    rows = N_HEADS * sq
    q_spec = pl.BlockSpec((1, sq, d), lambda b, j: (b, 0, 0))
    cache_spec = pl.BlockSpec((1, tkv, d), lambda b, j: (b, j, 0))
    new_spec = pl.BlockSpec((1, LANES, d), lambda b, j: (b, 0, 0))
    c_spec = pl.BlockSpec((1, N_HEADS, c_t.shape[2]), lambda b, j: (b, 0, 0))
    return pl.pallas_call(
        functools.partial(_sample_attn_kernel, sq=sq, tkv=tkv, past=past),
        grid=(nb, past // tkv),
        in_specs=[q_spec, cache_spec, cache_spec, new_spec, new_spec, c_spec],
        out_specs=q_spec,
        out_shape=jax.ShapeDtypeStruct(qb.shape, BF16),
        scratch_shapes=[pltpu.VMEM((rows, d), BF16), pltpu.VMEM((rows, 1), F32),
                        pltpu.VMEM((rows, 1), F32), pltpu.VMEM((rows, d), F32)],
        compiler_params=_params(("parallel", "arbitrary")),
        name="fox_attention_sample",
    )(qb, cache_k, cache_v, kn_pad, vn_pad, c_t)


def _row(v):
    return v.reshape(1, -1).astype(F32)


def _run_group(x, conv_in, past, wts, *, sgu_blk, tm_mix, tm_ffn, nsub, tm_proj):
    nb, s, d = x.shape
    mask = (np.arange(SGU_BLOCK)[None, :] // CHUNK) <= (np.arange(SGU_BLOCK)[:, None] // CHUNK)
    reps = SGU_BLOCK // sgu_blk
    m_blk = jnp.asarray(mask[:sgu_blk, :sgu_blk])
    ws_blk = jnp.where(m_blk[None], wts['a_w_s'][0][:, :sgu_blk, :sgu_blk], 0.0)
    eye = jnp.eye(reps, dtype=F32)
    ws = jnp.einsum('rs,gij->grisj', eye, ws_blk).reshape(SGU_GROUPS, SGU_BLOCK, SGU_BLOCK)
    bs = jnp.tile(wts['a_b_s'][0][:, :sgu_blk], (1, reps))
    bs_full = jnp.repeat(bs.T, SGU_GROUP_DIM, axis=1)
    mix = _sgu_mixer(x, _row(wts['norm_mix'][0]), wts['a_w_in'][0], _row(wts['a_ln_g'][0]),
                     _row(wts['a_ln_b'][0]), ws.astype(BF16), bs_full, wts['a_w_out'][0],
                     tm=tm_mix, emit_v=past is not None)
    if past is not None:
        h, sgu_v = mix
    else:
        h, sgu_v = mix, None

    def ffn(h, layer, att=None):
        return _conv_ffn(h, _row(wts['norm_ffn'][layer]), wts['f_w_up'][layer],
                         wts['f_conv_w'][layer], _row(wts['f_conv_b'][layer]),
                         wts['f_w_down'][layer], conv_in[layer], tm=tm_ffn, nsub=nsub,
                         att=att, w_o=None if att is None else wts['b_w_o'][0])

    h, conv0 = ffn(h, 0)

    proj = _kv_q_proj(
        h, _row(wts['kv_norm']), _row(wts['norm_mix'][1]), wts['w_k'], wts['w_v'], wts['b_w_q'][0],
        wts['w_f_pad'], wts['b_f_pad'], _row(jnp.tile(wts['k_norm_g'], N_HEADS)),
        _row(jnp.tile(wts['q_norm_g'][0], N_HEADS)), wts['head_sum'], wts['head_bcast'], tm=tm_proj,
        emit_bias=past is None)
    return h, sgu_v, conv0, proj, ffn


def kernel(x_prompt, x_sample, cache_k, cache_v, cache_logf, cache_ffn_conv, norm_mix, norm_ffn,
           a_w_in, a_ln_g, a_ln_b, a_w_s, a_b_s, a_w_out, f_w_up, f_conv_w, f_conv_b, f_w_down,
           kv_norm, w_k, w_v, k_norm_g, w_f, b_f, b_w_q, q_norm_g, b_w_o):
    batch, seq, d = x_prompt.shape
    dec_batch, dec_seq, _ = x_sample.shape
    past = cache_k.shape[1]
    head_of_col = np.arange(d) // HEAD_DIM
    head_sum = (head_of_col[:, None] == np.arange(LANES)[None, :]).astype(np.float32)
    wts = {
        'norm_mix': norm_mix, 'norm_ffn': norm_ffn,
        'a_w_in': a_w_in.astype(BF16), 'a_ln_g': a_ln_g, 'a_ln_b': a_ln_b, 'a_w_s': a_w_s,
        'a_b_s': a_b_s, 'a_w_out': a_w_out.astype(BF16),
        'f_w_up': f_w_up.astype(BF16), 'f_conv_w': f_conv_w, 'f_conv_b': f_conv_b,
        'f_w_down': f_w_down.astype(BF16),
        'kv_norm': kv_norm, 'w_k': w_k.astype(BF16), 'w_v': w_v.astype(BF16), 'k_norm_g': k_norm_g,
        'w_f_pad': jnp.pad(w_f, ((0, 0), (0, LANES - N_HEADS))).astype(BF16),
        'b_f_pad': jnp.pad(b_f, (0, LANES - N_HEADS)).reshape(1, LANES),
        'b_w_q': b_w_q.astype(BF16), 'q_norm_g': q_norm_g, 'b_w_o': b_w_o.astype(BF16),
        'head_sum': jnp.asarray(head_sum, BF16), 'head_bcast': jnp.asarray(head_sum.T, BF16),
    }

    zero_conv = jnp.zeros((DEPTH, batch, CONV_W - 1, 2 * D_FF), F32)
    h_p, _, conv_p0, (k_p, v_p, lf_p, qb, kb, vb, cp), ffn_p = _run_group(
        x_prompt, zero_conv, None, wts, sgu_blk=SGU_BLOCK, tm_mix=1024, tm_ffn=1024, nsub=1,
        tm_proj=512)
    att_p = _prompt_attention(qb, kb, vb, cp, tq=1024, tk=512)
    y_p, conv_p1 = ffn_p(h_p, 1, att_p)

    rows = dec_batch * dec_seq
    xs = x_sample.reshape(1, rows, d)
    h_s, sgu_v, conv_s0, (k_s, v_s, lf_s, qb, kb, vb), ffn_s = _run_group(
        xs, cache_ffn_conv, (cache_k, cache_v, cache_logf), wts, sgu_blk=min(SGU_BLOCK, dec_seq),
        tm_mix=rows, tm_ffn=rows, nsub=dec_batch, tm_proj=rows)
    lf_s = lf_s.reshape(dec_batch, dec_seq, N_HEADS)
    pad = LANES - dec_seq
    lf_all = jnp.concatenate(
        [cache_logf.astype(F32), lf_s,
         jnp.zeros((dec_batch, CUMSUM_TILE - dec_seq, N_HEADS), F32)], axis=1)
    c_s = _cumsum_lanes(lf_all.transpose(0, 2, 1).reshape(dec_batch * N_HEADS, -1))
    c_s = c_s.reshape(dec_batch, N_HEADS, -1)
    pad_new = lambda a: jnp.pad(a.reshape(dec_batch, dec_seq, d), ((0, 0), (0, pad), (0, 0)))
    att_s = _sample_attention(
        qb.reshape(dec_batch, dec_seq, d), cache_k.reshape(dec_batch, past, d),
        cache_v.reshape(dec_batch, past, d), pad_new(kb), pad_new(vb), c_s, tkv=1024)
    y_s, conv_s1 = ffn_s(h_s, 1, att_s.reshape(1, rows, d))

    heads = lambda a, b, s: a.reshape(b, s, N_HEADS, HEAD_DIM)
    return (y_p, y_s.reshape(dec_batch, dec_seq, d),
            sgu_v.reshape(N_A, dec_batch, dec_seq, d),
            jnp.stack([conv_p0, conv_p1]), jnp.stack([conv_s0, conv_s1]),
            heads(k_p, batch, seq), heads(v_p, batch, seq), lf_p,
            heads(k_s, dec_batch, dec_seq), heads(v_s, dec_batch, dec_seq), lf_s)
```

```python
import functools

import jax
import jax.numpy as jnp
import numpy as np
from jax import lax
from jax.experimental import pallas as pl
from jax.experimental.pallas import tpu as pltpu

D_MODEL = 1024
DEPTH = 2
N_A = DEPTH // 2
CHUNK = 64
SGU_BLOCK = 128
SGU_GROUPS = 4
SGU_GROUP_DIM = D_MODEL // SGU_GROUPS
D_FF = 2816
CONV_W = 3
N_HEADS = 16
HEAD_DIM = D_MODEL // N_HEADS
EPS = 1e-6

LANES = 128
SUBLANES = 8
MXU_K = 256
HEADS_PER_MXU_K = MXU_K // HEAD_DIM
HEADS_PER_LANE_TILE = LANES // HEAD_DIM
N_HEAD_PAIRS = N_HEADS // HEADS_PER_LANE_TILE
NEG_BIG = -1e30
LOG2E = 1.4426950408889634
N_BIAS_TERMS = 3
CHUNKS_PER_TRIP = 2
SGU_ROW_PARTS = 2
PROJ_ROW_PARTS = 2
VMEM_LIMIT = 56 * 1024 * 1024

F32 = jnp.float32
BF16 = jnp.bfloat16


def _const_spec(shape):
    nd = len(shape)
    return pl.BlockSpec(shape, lambda *_: (0,) * nd, pipeline_mode=pl.Buffered(1))


def _params(semantics):
    return pltpu.CompilerParams(dimension_semantics=semantics, vmem_limit_bytes=VMEM_LIMIT)


def _rms_hat(x):
    return x * lax.rsqrt(jnp.mean(x * x, axis=-1, keepdims=True) + EPS)


def _split2(x):
    hi = x.astype(BF16)
    lo = (x - hi.astype(F32)).astype(BF16)
    return hi, lo


def _split3(x):
    hi = x.astype(BF16)
    r = x - hi.astype(F32)
    mid = r.astype(BF16)
    lo = (r - mid.astype(F32)).astype(BF16)
    return hi, mid, lo


def _dot(a, b):
    return jnp.dot(a, b, preferred_element_type=F32)


def _dot_nt(a, b):
    return lax.dot_general(a, b, (((1,), (1,)), ((), ())), preferred_element_type=F32)


def _sgu_kernel(x_ref, g_ref, win_ref, lng_ref, lnb_ref, ws_ref, bs_ref, wout_ref,
                h_ref, *rest, tm, emit_v):
    if emit_v:
        v_ref, gated_ref = rest
    else:
        (gated_ref,) = rest
    part = tm // SGU_ROW_PARTS
    assert part % SGU_BLOCK == 0
    bias = bs_ref[...]
    staged = []
    for pt in range(SGU_ROW_PARTS):
        x = x_ref[0, pt * part:(pt + 1) * part, :]
        hn = (_rms_hat(x) * g_ref[...]).astype(BF16)
        staged.append((x, _dot(hn, win_ref[:, :D_MODEL]), _dot(hn, win_ref[:, D_MODEL:])))
    for pt, (x, zu, zv) in enumerate(staged):
        u = jax.nn.gelu(zu, approximate=True)
        v = jax.nn.gelu(zv, approximate=True)
        vc = v - jnp.mean(v, axis=-1, keepdims=True)
        v = vc * lax.rsqrt(jnp.mean(vc * vc, axis=-1, keepdims=True) + EPS)
        v = v * lng_ref[...] + lnb_ref[...]
        if emit_v:
            v_ref[0, pt * part:(pt + 1) * part, :] = v
        vb = v.astype(BF16)
        for r in range(part // SGU_BLOCK):
            rows = slice(r * SGU_BLOCK, (r + 1) * SGU_BLOCK)
            out_rows = slice(pt * part + r * SGU_BLOCK, pt * part + (r + 1) * SGU_BLOCK)
            for g in range(SGU_GROUPS):
                cols = slice(g * SGU_GROUP_DIM, (g + 1) * SGU_GROUP_DIM)
                mixed = _dot(ws_ref[g], vb[rows, cols]) + bias[:, cols]
                gated_ref[out_rows, cols] = (u[rows, cols] * mixed).astype(BF16)
        part_rows = slice(pt * part, (pt + 1) * part)
        h_ref[0, part_rows, :] = x + _dot(gated_ref[part_rows, :], wout_ref[...])


def _sgu_mixer(x, g, w_in, ln_g, ln_b, ws, bs_full, w_out, *, tm, emit_v):
    nb, s, d = x.shape
    tok_spec = pl.BlockSpec((1, tm, d), lambda b, t: (b, t, 0))
    out_shape = [jax.ShapeDtypeStruct(x.shape, F32)]
    out_specs = [tok_spec]
    if emit_v:
        out_shape.append(jax.ShapeDtypeStruct(x.shape, F32))
        out_specs.append(tok_spec)
    res = pl.pallas_call(
        functools.partial(_sgu_kernel, tm=tm, emit_v=emit_v),
        grid=(nb, s // tm),
        in_specs=[tok_spec, _const_spec(g.shape), _const_spec(w_in.shape), _const_spec(ln_g.shape),
                  _const_spec(ln_b.shape), _const_spec(ws.shape), _const_spec(bs_full.shape),
                  _const_spec(w_out.shape)],
        out_specs=out_specs,
        out_shape=out_shape,
        scratch_shapes=[pltpu.VMEM((tm, d), BF16)],
        compiler_params=_params(("parallel", "parallel")),
        name="sgu_mixer",
    )(x, g, w_in, ln_g, ln_b, ws, bs_full, w_out)
    return res if emit_v else res[0]


def _ff_chunks():
    width, chunks, start = 512, [], 0
    while start < D_FF:
        w = min(width, D_FF - start)
        chunks.append((start, w))
        start += w
    return chunks


def _causal_conv3(a, prev, w, b):
    row = lax.broadcasted_iota(jnp.int32, (SUBLANES, a.shape[1]), 0)

    def shifted(k):
        r = pltpu.roll(a, k, 0)
        head = r[:SUBLANES]
        for j in range(k):
            head = jnp.where(row == j, prev[CONV_W - 1 - k + j:CONV_W - k + j], head)
        return jnp.concatenate([head, r[SUBLANES:]], axis=0)

    return shifted(2) * w[0:1] + shifted(1) * w[1:2] + a * w[2:3] + b


def _ffn_kernel(*refs, tm, nsub, with_oproj):
    if with_oproj:
        (x_ref, att_ref, wo_ref, g_ref, wup_ref, cw_ref, cb_ref, wdn_ref, st_ref,
         o_ref, ns_ref, carry_ref) = refs
    else:
        (x_ref, g_ref, wup_ref, cw_ref, cb_ref, wdn_ref, st_ref,
         o_ref, ns_ref, carry_ref) = refs
    seg = tm // nsub

    @pl.when(pl.program_id(1) == 0)
    def _():
        carry_ref[...] = st_ref[...]

    x = x_ref[0]
    if with_oproj:
        x = x + _dot(att_ref[0], wo_ref[...])
    hn = (_rms_hat(x) * g_ref[...]).astype(BF16)
    acc = x
    chunks = _ff_chunks()

    def up_proj(ci):
        start, width = chunks[ci]
        return [_dot(hn, wup_ref[:, off:off + width]) for off in (start, D_FF + start)]

    a_next = up_proj(0)
    for ci, (start, width) in enumerate(chunks):
        a_cur = a_next
        if ci + 1 < len(chunks):
            a_next = up_proj(ci + 1)
        halves = []
        for a, off in zip(a_cur, (start, D_FF + start)):
            cols = slice(off, off + width)
            w = cw_ref[:, cols]
            b = cb_ref[:, cols]
            pieces = []
            for q in range(nsub):
                a_q = a[q * seg:(q + 1) * seg]
                pieces.append(_causal_conv3(a_q, carry_ref[q, :, cols], w, b))
                carry_ref[q, :, cols] = a_q[seg - (CONV_W - 1):]
            halves.append(pieces[0] if nsub == 1 else jnp.concatenate(pieces, axis=0))
        gate, val = halves
        act = (gate * jax.nn.sigmoid(gate) * val).astype(BF16)
        acc = acc + _dot(act, wdn_ref[start:start + width, :])
    o_ref[0] = acc
    ns_ref[...] = carry_ref[...]


def _conv_ffn(x, g, w_up, conv_w, conv_b, w_down, state, *, tm, nsub, att=None, w_o=None):
    nb, s, d = x.shape
    with_oproj = att is not None
    tok_spec = pl.BlockSpec((1, tm, d), lambda b, t: (b, t, 0))
    st_spec = pl.BlockSpec((nsub, CONV_W - 1, 2 * D_FF), lambda b, t: (b, 0, 0))
    operands = [x]
    in_specs = [tok_spec]
    if with_oproj:
        operands += [att, w_o]
        in_specs += [tok_spec, _const_spec(w_o.shape)]
    operands += [g, w_up, conv_w, conv_b, w_down, state]
    in_specs += [_const_spec(g.shape), _const_spec(w_up.shape), _const_spec(conv_w.shape),
                 _const_spec(conv_b.shape), _const_spec(w_down.shape), st_spec]
    return pl.pallas_call(
        functools.partial(_ffn_kernel, tm=tm, nsub=nsub, with_oproj=with_oproj),
        grid=(nb, s // tm),
        in_specs=in_specs,
        out_specs=[tok_spec, st_spec],
        out_shape=[jax.ShapeDtypeStruct(x.shape, F32), jax.ShapeDtypeStruct(state.shape, F32)],
        scratch_shapes=[pltpu.VMEM((nsub, CONV_W - 1, 2 * D_FF), F32)],
        compiler_params=_params(("parallel", "arbitrary")),
        name="conv_ffn",
    )(*operands)


def _head_rms(x, e_ref, et_ref, g):
    ss = _dot((x * x).astype(BF16), e_ref[...])
    r = lax.rsqrt(ss * (1.0 / HEAD_DIM) + EPS)
    rhi, rlo = _split2(r)
    rb = _dot(rhi, et_ref[...]) + _dot(rlo, et_ref[...])
    return x * rb * g


def _proj_kernel(x_ref, gkv_ref, gq_ref, wk_ref, wv_ref, wq_ref, wf_ref, bf_ref, kg_ref, qg_ref,
                 e_ref, et_ref, *rest, tm, emit_bias):
    if emit_bias:
        (tril_ref, place_ref, k_ref, v_ref, lf_ref, qb_ref, kb_ref, vb_ref, cp_ref, carry_ref) = rest
    else:
        (k_ref, v_ref, lf_ref, qb_ref, kb_ref, vb_ref) = rest
    part = tm // PROJ_ROW_PARTS
    staged = []
    for pt in range(PROJ_ROW_PARTS):
        xh = _rms_hat(x_ref[0, pt * part:(pt + 1) * part, :])
        hkv = (xh * gkv_ref[...]).astype(BF16)
        hq = (xh * gq_ref[...]).astype(BF16)
        staged.append((_dot(hkv, wk_ref[...]), _dot(hkv, wv_ref[...]),
                       _dot(hkv, wf_ref[...]) + bf_ref[...],
                       _dot(hq, wq_ref[...])))
    if emit_bias:
        @pl.when(pl.program_id(1) == 0)
        def _():
            carry_ref[...] = jnp.zeros_like(carry_ref)

    for pt, (k_raw, v, f, q_raw) in enumerate(staged):
        rows = slice(pt * part, (pt + 1) * part)
        k = _head_rms(k_raw, e_ref, et_ref, kg_ref[...])
        k_ref[0, rows, :] = k
        kb_ref[0, rows, :] = k.astype(BF16)
        v_ref[0, rows, :] = v
        vb_ref[0, rows, :] = v.astype(BF16)
        lf = jax.nn.log_sigmoid(f)
        lf_ref[0, rows, :] = lf[:, :N_HEADS]
        q = _head_rms(q_raw, e_ref, et_ref, qg_ref[...])
        qb_ref[0, rows, :] = (q * (LOG2E * HEAD_DIM ** -0.5)).astype(BF16)
        if emit_bias:
            c = carry_ref[0:1, :]
            for piece in _split3(lf):
                c = c + _dot(tril_ref[...], piece)
            carry_ref[0:1, :] = c[part - 1:, :]
            cp = None
            for j, piece in enumerate(_split3(c * (-LOG2E))):
                term = _dot(piece, place_ref[j])
                cp = term if cp is None else cp + term
            cp_ref[0, rows, :] = cp.astype(BF16)


def _bias_placement():
    place = np.zeros((N_BIAS_TERMS, LANES, D_MODEL), np.float32)
    for h in range(N_HEADS):
        base = (h // HEADS_PER_LANE_TILE) * LANES + (HEAD_DIM if h % HEADS_PER_LANE_TILE == 0 else 0)
        for j in range(N_BIAS_TERMS):
            place[j, h, base + j] = 1.0
    return jnp.asarray(place, BF16)


def _kv_q_proj(x, g_kv, g_q, w_k, w_v, w_q, w_f, b_f, k_g, q_g, e, et, *, tm, emit_bias):
    nb, s, d = x.shape
    tok_spec = pl.BlockSpec((1, tm, d), lambda b, t: (b, t, 0))
    lf_spec = pl.BlockSpec((1, tm, N_HEADS), lambda b, t: (b, t, 0))
    consts = [g_kv, g_q, w_k, w_v, w_q, w_f, b_f, k_g, q_g, e, et]
    out_specs = [tok_spec, tok_spec, lf_spec, tok_spec, tok_spec, tok_spec]
    out_shape = [jax.ShapeDtypeStruct(x.shape, F32), jax.ShapeDtypeStruct(x.shape, F32),
                 jax.ShapeDtypeStruct((nb, s, N_HEADS), F32),
                 jax.ShapeDtypeStruct(x.shape, BF16), jax.ShapeDtypeStruct(x.shape, BF16),
                 jax.ShapeDtypeStruct(x.shape, BF16)]
    scratch = []
    if emit_bias:
        part = tm // PROJ_ROW_PARTS
        consts += [jnp.asarray(np.tril(np.ones((part, part), np.float32)), BF16), _bias_placement()]
        out_specs.append(tok_spec)
        out_shape.append(jax.ShapeDtypeStruct(x.shape, BF16))
        scratch.append(pltpu.VMEM((SUBLANES, LANES), F32))
    return pl.pallas_call(
        functools.partial(_proj_kernel, tm=tm, emit_bias=emit_bias),
        grid=(nb, s // tm),
        in_specs=[tok_spec] + [_const_spec(c.shape) for c in consts],
        out_specs=out_specs,
        out_shape=out_shape,
        scratch_shapes=scratch,
        compiler_params=_params(("parallel", "arbitrary")),
        name="kv_q_proj",
    )(x, *consts)


CUMSUM_TILE = 512


def _cumsum_kernel(x_ref, tri_ref, o_ref, carry_ref):
    @pl.when(pl.program_id(0) == 0)
    def _():
        carry_ref[...] = jnp.zeros_like(carry_ref)

    tri = tri_ref[...]
    out = carry_ref[...]
    for part in _split3(x_ref[...]):
        out = out + _dot(part, tri)
    o_ref[...] = out
    carry_ref[...] = jnp.broadcast_to(out[:, CUMSUM_TILE - 1:], out.shape)


def _cumsum_lanes(x):
    r, s = x.shape
    assert s % CUMSUM_TILE == 0, s
    tri = jnp.asarray(np.triu(np.ones((CUMSUM_TILE, CUMSUM_TILE), np.float32)), BF16)
    spec = pl.BlockSpec((r, CUMSUM_TILE), lambda t: (0, t))
    return pl.pallas_call(
        _cumsum_kernel,
        grid=(s // CUMSUM_TILE,),
        in_specs=[spec, _const_spec(tri.shape)],
        out_specs=spec,
        out_shape=jax.ShapeDtypeStruct(x.shape, F32),
        scratch_shapes=[pltpu.VMEM((r, CUMSUM_TILE), F32)],
        compiler_params=_params(("arbitrary",)),
        name="logf_cumsum",
    )(x, tri)


def _attn_kernel(q_ref, k_ref, v_ref, cp_ref, o_ref, *, tq, tk):
    i = pl.program_id(2)
    q = q_ref[0]
    lane_q = lax.broadcasted_iota(jnp.int32, q.shape, 1)
    first_q = lane_q < HEAD_DIM
    ones_hi = jnp.where((lane_q >= HEAD_DIM) & (lane_q < HEAD_DIM + N_BIAS_TERMS), 1.0, 0.0)
    ones_hi = ones_hi.astype(BF16)
    ones_lo = jnp.where(lane_q < N_BIAS_TERMS, 1.0, 0.0).astype(BF16)
    qs = (jnp.where(first_q, q, ones_hi), jnp.where(first_q, ones_lo, q))
    first_k = lax.broadcasted_iota(jnp.int32, (tk, LANES), 1) < HEAD_DIM
    q0 = pl.multiple_of(i * tq, tq)

    def logits(k0, row0, masked):
        k = k_ref[0, pl.ds(k0, tk), :]
        cp = cp_ref[0, pl.ds(k0, tk), :]
        ks = (jnp.where(first_k, k, cp), jnp.where(first_k, cp, k))
        ss = [_dot_nt(qs[hh][row0:], ks[hh]) for hh in range(HEADS_PER_LANE_TILE)]
        if masked is not None:
            ss = [jnp.where(masked, s, NEG_BIG) for s in ss]
        return ss

    def update(states, ss, k0, row0):
        v = v_ref[0, pl.ds(k0, tk), :]
        one_v = jnp.ones_like(v)
        vs = (jnp.where(first_k, v, one_v), jnp.where(first_k, one_v, v))
        out = []
        for hh in range(HEADS_PER_LANE_TILE):
            m_all, acc_all = states[hh]
            m, acc = m_all[row0:], acc_all[row0:]
            m_new = jnp.maximum(m, jnp.max(ss[hh], axis=-1, keepdims=True))
            p = jnp.exp2(ss[hh] - m_new).astype(BF16)
            acc = jnp.exp2(m - m_new) * acc + _dot(p, vs[hh])
            if row0:
                m_new = jnp.concatenate([m_all[:row0], m_new], axis=0)
                acc = jnp.concatenate([acc_all[:row0], acc], axis=0)
            out.append((m_new, acc))
        return tuple(out)

    def chunk_group(states, starts, row0s, masks):
        sss = [logits(k0, r0, mk) for k0, r0, mk in zip(starts, row0s, masks)]
        for ss, k0, r0 in zip(sss, starts, row0s):
            states = update(states, ss, k0, r0)
        return states

    per_block = tq // tk
    assert per_block % CHUNKS_PER_TRIP == 0
    init = tuple((jnp.full((tq, 1), NEG_BIG, F32), jnp.zeros((tq, LANES), F32))
                 for _ in range(HEADS_PER_LANE_TILE))

    def full_trip(jj, st):
        starts = [pl.multiple_of((jj * CHUNKS_PER_TRIP + u) * tk, tk) for u in range(CHUNKS_PER_TRIP)]
        return chunk_group(st, starts, [0] * CHUNKS_PER_TRIP, [None] * CHUNKS_PER_TRIP)

    states = lax.fori_loop(0, i * (per_block // CHUNKS_PER_TRIP), full_trip, init)
    for d0 in range(0, per_block, CHUNKS_PER_TRIP):
        ds = range(d0, d0 + CHUNKS_PER_TRIP)
        masks = []
        for d in ds:
            row = lax.broadcasted_iota(jnp.int32, (tq - d * tk, tk), 0)
            col = lax.broadcasted_iota(jnp.int32, (tq - d * tk, tk), 1)
            masks.append(col <= row)
        states = chunk_group(states, [pl.multiple_of(q0 + d * tk, tk) for d in ds],
                             [d * tk for d in ds], masks)
    outs = [acc / pltpu.roll(acc, HEAD_DIM, 1) for (_, acc) in states]
    o_ref[0] = jnp.where(first_q, outs[0], outs[1]).astype(o_ref.dtype)


def _prompt_attention(qb, kb, vb, cp, *, tq, tk):
    nb, s, d = qb.shape
    q_spec = pl.BlockSpec((1, tq, LANES), lambda b, p, i: (b, i, p))
    kv_spec = pl.BlockSpec((1, s, LANES), lambda b, p, i: (b, 0, p))
    return pl.pallas_call(
        functools.partial(_attn_kernel, tq=tq, tk=tk),
        grid=(nb, N_HEAD_PAIRS, s // tq),
        in_specs=[q_spec, kv_spec, kv_spec, kv_spec],
        out_specs=q_spec,
        out_shape=jax.ShapeDtypeStruct(qb.shape, BF16),
        compiler_params=_params(("parallel", "parallel", "parallel")),
        name="fox_attention_prompt",
    )(qb, kb, vb, cp)


def _sample_attn_kernel(q_ref, kt_ref, vt_ref, kn_ref, vn_ref, c_ref, o_ref, *, sq, past):
    q = q_ref[0]
    lane_head = lax.broadcasted_iota(jnp.int32, q.shape, 1) // HEAD_DIM
    qst = jnp.concatenate([jnp.where(lane_head == hl, q, jnp.zeros_like(q))
                           for hl in range(HEADS_PER_MXU_K)], axis=0)
    c = c_ref[0, 0]
    c_row0 = c[:, past:past + 1]

    def bias_rows(ck):
        b = (c_row0 - ck) * LOG2E
        return jnp.concatenate([jnp.broadcast_to(b[hl:hl + 1], (sq, b.shape[1]))
                                for hl in range(HEADS_PER_MXU_K)], axis=0)

    s_old = _dot(qst, kt_ref[0].astype(BF16)) + bias_rows(c[:, :past])
    s_new = _dot_nt(qst, kn_ref[0]) + bias_rows(c[:, past:past + LANES])
    t = lax.broadcasted_iota(jnp.int32, s_new.shape, 0) % sq
    col = lax.broadcasted_iota(jnp.int32, s_new.shape, 1)
    s_new = jnp.where(col <= t, s_new, NEG_BIG)
    m = jnp.maximum(jnp.max(s_old, axis=-1, keepdims=True), jnp.max(s_new, axis=-1, keepdims=True))
    p_old = jnp.exp2(s_old - m)
    p_new = jnp.exp2(s_new - m)
    l = jnp.sum(p_old, axis=-1, keepdims=True) + jnp.sum(p_new, axis=-1, keepdims=True)
    o = _dot_nt(p_old.astype(BF16), vt_ref[0].astype(BF16)) + _dot(p_new.astype(BF16), vn_ref[0])
    o = o / l
    out = jnp.zeros(q.shape, F32)
    for hl in range(HEADS_PER_MXU_K):
        out = jnp.where(lane_head == hl, o[hl * sq:(hl + 1) * sq], out)
    o_ref[0] = out.astype(o_ref.dtype)


def _sample_attention(qb, cache_kt, cache_vt, kn_pad, vn_pad, c_t):
    nb, sq, d = qb.shape
    past = cache_kt.shape[2]
    n_grp = N_HEADS // HEADS_PER_MXU_K
    c_grp = c_t.reshape(nb, n_grp, HEADS_PER_MXU_K, c_t.shape[2])
    q_spec = pl.BlockSpec((1, sq, MXU_K), lambda b, g: (b, 0, g))
    cache_spec = pl.BlockSpec((1, MXU_K, past), lambda b, g: (b, g, 0))
    new_spec = pl.BlockSpec((1, LANES, MXU_K), lambda b, g: (b, 0, g))
    c_spec = pl.BlockSpec((1, 1, HEADS_PER_MXU_K, c_t.shape[2]), lambda b, g: (b, g, 0, 0))
    return pl.pallas_call(
        functools.partial(_sample_attn_kernel, sq=sq, past=past),
        grid=(nb, n_grp),
        in_specs=[q_spec, cache_spec, cache_spec, new_spec, new_spec, c_spec],
        out_specs=q_spec,
        out_shape=jax.ShapeDtypeStruct(qb.shape, BF16),
        compiler_params=_params(("parallel", "parallel")),
        name="fox_attention_sample",
    )(qb, cache_kt, cache_vt, kn_pad, vn_pad, c_grp)


def _row(v):
    return v.reshape(1, -1).astype(F32)


def _run_group(x, conv_in, past, wts, *, sgu_blk, tm_mix, tm_ffn, nsub, tm_proj):
    nb, s, d = x.shape
    mask = (np.arange(SGU_BLOCK)[None, :] // CHUNK) <= (np.arange(SGU_BLOCK)[:, None] // CHUNK)
    reps = SGU_BLOCK // sgu_blk
    m_blk = jnp.asarray(mask[:sgu_blk, :sgu_blk])
    ws_blk = jnp.where(m_blk[None], wts['a_w_s'][0][:, :sgu_blk, :sgu_blk], 0.0)
    eye = jnp.eye(reps, dtype=F32)
    ws = jnp.einsum('rs,gij->grisj', eye, ws_blk).reshape(SGU_GROUPS, SGU_BLOCK, SGU_BLOCK)
    bs = jnp.tile(wts['a_b_s'][0][:, :sgu_blk], (1, reps))
    bs_full = jnp.repeat(bs.T, SGU_GROUP_DIM, axis=1)
    mix = _sgu_mixer(x, _row(wts['norm_mix'][0]), wts['a_w_in'][0], _row(wts['a_ln_g'][0]),
                     _row(wts['a_ln_b'][0]), ws.astype(BF16), bs_full, wts['a_w_out'][0],
                     tm=tm_mix, emit_v=past is not None)
    if past is not None:
        h, sgu_v = mix
    else:
        h, sgu_v = mix, None

    def ffn(h, layer, att=None):
        return _conv_ffn(h, _row(wts['norm_ffn'][layer]), wts['f_w_up'][layer],
                         wts['f_conv_w'][layer], _row(wts['f_conv_b'][layer]),
                         wts['f_w_down'][layer], conv_in[layer], tm=tm_ffn, nsub=nsub,
                         att=att, w_o=None if att is None else wts['b_w_o'][0])

    h, conv0 = ffn(h, 0)

    proj = _kv_q_proj(
        h, _row(wts['kv_norm']), _row(wts['norm_mix'][1]), wts['w_k'], wts['w_v'], wts['b_w_q'][0],
        wts['w_f_pad'], wts['b_f_pad'], _row(jnp.tile(wts['k_norm_g'], N_HEADS)),
        _row(jnp.tile(wts['q_norm_g'][0], N_HEADS)), wts['head_sum'], wts['head_bcast'], tm=tm_proj,
        emit_bias=past is None)
    return h, sgu_v, conv0, proj, ffn


def kernel(x_prompt, x_sample, cache_k, cache_v, cache_logf, cache_ffn_conv, norm_mix, norm_ffn,
           a_w_in, a_ln_g, a_ln_b, a_w_s, a_b_s, a_w_out, f_w_up, f_conv_w, f_conv_b, f_w_down,
           kv_norm, w_k, w_v, k_norm_g, w_f, b_f, b_w_q, q_norm_g, b_w_o):
    batch, seq, d = x_prompt.shape
    dec_batch, dec_seq, _ = x_sample.shape
    past = cache_k.shape[1]
    head_of_col = np.arange(d) // HEAD_DIM
    head_sum = (head_of_col[:, None] == np.arange(LANES)[None, :]).astype(np.float32)
    wts = {
        'norm_mix': norm_mix, 'norm_ffn': norm_ffn,
        'a_w_in': a_w_in.astype(BF16), 'a_ln_g': a_ln_g, 'a_ln_b': a_ln_b, 'a_w_s': a_w_s,
        'a_b_s': a_b_s, 'a_w_out': a_w_out.astype(BF16),
        'f_w_up': f_w_up.astype(BF16), 'f_conv_w': f_conv_w, 'f_conv_b': f_conv_b,
        'f_w_down': f_w_down.astype(BF16),
        'kv_norm': kv_norm, 'w_k': w_k.astype(BF16), 'w_v': w_v.astype(BF16), 'k_norm_g': k_norm_g,
        'w_f_pad': jnp.pad(w_f, ((0, 0), (0, LANES - N_HEADS))).astype(BF16),
        'b_f_pad': jnp.pad(b_f, (0, LANES - N_HEADS)).reshape(1, LANES),
        'b_w_q': b_w_q.astype(BF16), 'q_norm_g': q_norm_g, 'b_w_o': b_w_o.astype(BF16),
        'head_sum': jnp.asarray(head_sum, BF16), 'head_bcast': jnp.asarray(head_sum.T, BF16),
    }

    zero_conv = jnp.zeros((DEPTH, batch, CONV_W - 1, 2 * D_FF), F32)
    h_p, _, conv_p0, (k_p, v_p, lf_p, qb, kb, vb, cp), ffn_p = _run_group(
        x_prompt, zero_conv, None, wts, sgu_blk=SGU_BLOCK, tm_mix=1024, tm_ffn=1024, nsub=1,
        tm_proj=512)
    att_p = _prompt_attention(qb, kb, vb, cp, tq=1024, tk=512)
    y_p, conv_p1 = ffn_p(h_p, 1, att_p)

    rows = dec_batch * dec_seq
    xs = x_sample.reshape(1, rows, d)
    h_s, sgu_v, conv_s0, (k_s, v_s, lf_s, qb, kb, vb), ffn_s = _run_group(
        xs, cache_ffn_conv, (cache_k, cache_v, cache_logf), wts, sgu_blk=min(SGU_BLOCK, dec_seq),
        tm_mix=rows, tm_ffn=rows, nsub=dec_batch, tm_proj=rows)
    lf_s = lf_s.reshape(dec_batch, dec_seq, N_HEADS)
    pad = LANES - dec_seq
    lf_all = jnp.concatenate(
        [cache_logf.astype(F32), lf_s,
         jnp.zeros((dec_batch, CUMSUM_TILE - dec_seq, N_HEADS), F32)], axis=1)
    c_s = _cumsum_lanes(lf_all.transpose(0, 2, 1).reshape(dec_batch * N_HEADS, -1))
    c_s = c_s.reshape(dec_batch, N_HEADS, -1)
    head_major = lambda a: a.transpose(0, 2, 3, 1).reshape(dec_batch, d, past)
    pad_new = lambda a: jnp.pad(a.reshape(dec_batch, dec_seq, d), ((0, 0), (0, pad), (0, 0)))
    att_s = _sample_attention(
        qb.reshape(dec_batch, dec_seq, d), head_major(cache_k), head_major(cache_v),
        pad_new(kb), pad_new(vb), c_s)
    y_s, conv_s1 = ffn_s(h_s, 1, att_s.reshape(1, rows, d))

    heads = lambda a, b, s: a.reshape(b, s, N_HEADS, HEAD_DIM)
    return (y_p, y_s.reshape(dec_batch, dec_seq, d),
            sgu_v.reshape(N_A, dec_batch, dec_seq, d),
            jnp.stack([conv_p0, conv_p1]), jnp.stack([conv_s0, conv_s1]),
            heads(k_p, batch, seq), heads(v_p, batch, seq), lf_p,
            heads(k_s, dec_batch, dec_seq), heads(v_s, dec_batch, dec_seq), lf_s)
```

```python
import functools

import jax
import jax.numpy as jnp
import numpy as np
from jax import lax
from jax.experimental import pallas as pl
from jax.experimental.pallas import tpu as pltpu

D_MODEL = 1024
DEPTH = 2
N_A = DEPTH // 2
CHUNK = 64
SGU_BLOCK = 128
SGU_GROUPS = 4
SGU_GROUP_DIM = D_MODEL // SGU_GROUPS
D_FF = 2816
CONV_W = 3
N_HEADS = 16
HEAD_DIM = D_MODEL // N_HEADS
EPS = 1e-6

LANES = 128
SUBLANES = 8
MXU_K = 256
HEADS_PER_MXU_K = MXU_K // HEAD_DIM
HEADS_PER_LANE_TILE = LANES // HEAD_DIM
N_HEAD_PAIRS = N_HEADS // HEADS_PER_LANE_TILE
NEG_BIG = -1e30
LOG2E = 1.4426950408889634
N_BIAS_TERMS = 3
CHUNKS_PER_TRIP = 2
QK_BOUND_SLACK = 1.05
EXP2_DEAD = 160.0
SGU_ROW_PARTS = 2
PROJ_ROW_PARTS = 2
VMEM_LIMIT = 56 * 1024 * 1024

F32 = jnp.float32
BF16 = jnp.bfloat16


def _const_spec(shape):
    nd = len(shape)
    return pl.BlockSpec(shape, lambda *_: (0,) * nd, pipeline_mode=pl.Buffered(1))


def _params(semantics):
    return pltpu.CompilerParams(dimension_semantics=semantics, vmem_limit_bytes=VMEM_LIMIT)


def _rms_hat(x):
    return x * lax.rsqrt(jnp.mean(x * x, axis=-1, keepdims=True) + EPS)


def _split2(x):
    hi = x.astype(BF16)
    lo = (x - hi.astype(F32)).astype(BF16)
    return hi, lo


def _split3(x):
    hi = x.astype(BF16)
    r = x - hi.astype(F32)
    mid = r.astype(BF16)
    lo = (r - mid.astype(F32)).astype(BF16)
    return hi, mid, lo


def _dot(a, b):
    return jnp.dot(a, b, preferred_element_type=F32)


def _dot_nt(a, b):
    return lax.dot_general(a, b, (((1,), (1,)), ((), ())), preferred_element_type=F32)


def _sgu_kernel(x_ref, g_ref, win_ref, lng_ref, lnb_ref, ws_ref, bs_ref, wout_ref,
                h_ref, *rest, tm, emit_v):
    if emit_v:
        v_ref, gated_ref = rest
    else:
        (gated_ref,) = rest
    part = tm // SGU_ROW_PARTS
    assert part % SGU_BLOCK == 0
    bias = bs_ref[...]
    staged = []
    for pt in range(SGU_ROW_PARTS):
        x = x_ref[0, pt * part:(pt + 1) * part, :]
        hn = (_rms_hat(x) * g_ref[...]).astype(BF16)
        staged.append((x, _dot(hn, win_ref[:, :D_MODEL]), _dot(hn, win_ref[:, D_MODEL:])))
    for pt, (x, zu, zv) in enumerate(staged):
        u = jax.nn.gelu(zu, approximate=True)
        v = jax.nn.gelu(zv, approximate=True)
        vc = v - jnp.mean(v, axis=-1, keepdims=True)
        v = vc * lax.rsqrt(jnp.mean(vc * vc, axis=-1, keepdims=True) + EPS)
        v = v * lng_ref[...] + lnb_ref[...]
        if emit_v:
            v_ref[0, pt * part:(pt + 1) * part, :] = v
        vb = v.astype(BF16)
        for r in range(part // SGU_BLOCK):
            rows = slice(r * SGU_BLOCK, (r + 1) * SGU_BLOCK)
            out_rows = slice(pt * part + r * SGU_BLOCK, pt * part + (r + 1) * SGU_BLOCK)
            for g in range(SGU_GROUPS):
                cols = slice(g * SGU_GROUP_DIM, (g + 1) * SGU_GROUP_DIM)
                mixed = _dot(ws_ref[g], vb[rows, cols]) + bias[:, cols]
                gated_ref[out_rows, cols] = (u[rows, cols] * mixed).astype(BF16)
        part_rows = slice(pt * part, (pt + 1) * part)
        h_ref[0, part_rows, :] = x + _dot(gated_ref[part_rows, :], wout_ref[...])


def _sgu_mixer(x, g, w_in, ln_g, ln_b, ws, bs_full, w_out, *, tm, emit_v):
    nb, s, d = x.shape
    tok_spec = pl.BlockSpec((1, tm, d), lambda b, t: (b, t, 0))
    out_shape = [jax.ShapeDtypeStruct(x.shape, F32)]
    out_specs = [tok_spec]
    if emit_v:
        out_shape.append(jax.ShapeDtypeStruct(x.shape, F32))
        out_specs.append(tok_spec)
    res = pl.pallas_call(
        functools.partial(_sgu_kernel, tm=tm, emit_v=emit_v),
        grid=(nb, s // tm),
        in_specs=[tok_spec, _const_spec(g.shape), _const_spec(w_in.shape), _const_spec(ln_g.shape),
                  _const_spec(ln_b.shape), _const_spec(ws.shape), _const_spec(bs_full.shape),
                  _const_spec(w_out.shape)],
        out_specs=out_specs,
        out_shape=out_shape,
        scratch_shapes=[pltpu.VMEM((tm, d), BF16)],
        compiler_params=_params(("parallel", "parallel")),
        name="sgu_mixer",
    )(x, g, w_in, ln_g, ln_b, ws, bs_full, w_out)
    return res if emit_v else res[0]


def _ff_chunks():
    width, chunks, start = 512, [], 0
    while start < D_FF:
        w = min(width, D_FF - start)
        chunks.append((start, w))
        start += w
    return chunks


def _causal_conv3(a, prev, w, b):
    row = lax.broadcasted_iota(jnp.int32, (SUBLANES, a.shape[1]), 0)

    def shifted(k):
        r = pltpu.roll(a, k, 0)
        head = r[:SUBLANES]
        for j in range(k):
            head = jnp.where(row == j, prev[CONV_W - 1 - k + j:CONV_W - k + j], head)
        return jnp.concatenate([head, r[SUBLANES:]], axis=0)

    return shifted(2) * w[0:1] + shifted(1) * w[1:2] + a * w[2:3] + b


def _ffn_kernel(*refs, tm, nsub, with_oproj):
    if with_oproj:
        (x_ref, att_ref, wo_ref, g_ref, wup_ref, cw_ref, cb_ref, wdn_ref, st_ref,
         o_ref, ns_ref, carry_ref) = refs
    else:
        (x_ref, g_ref, wup_ref, cw_ref, cb_ref, wdn_ref, st_ref,
         o_ref, ns_ref, carry_ref) = refs
    seg = tm // nsub

    @pl.when(pl.program_id(1) == 0)
    def _():
        carry_ref[...] = st_ref[...]

    x = x_ref[0]
    if with_oproj:
        x = x + _dot(att_ref[0], wo_ref[...])
    hn = (_rms_hat(x) * g_ref[...]).astype(BF16)
    acc = x
    chunks = _ff_chunks()

    def up_proj(ci):
        start, width = chunks[ci]
        return [_dot(hn, wup_ref[:, off:off + width]) for off in (start, D_FF + start)]

    a_next = up_proj(0)
    for ci, (start, width) in enumerate(chunks):
        a_cur = a_next
        if ci + 1 < len(chunks):
            a_next = up_proj(ci + 1)
        halves = []
        for a, off in zip(a_cur, (start, D_FF + start)):
            cols = slice(off, off + width)
            w = cw_ref[:, cols]
            b = cb_ref[:, cols]
            pieces = []
            for q in range(nsub):
                a_q = a[q * seg:(q + 1) * seg]
                pieces.append(_causal_conv3(a_q, carry_ref[q, :, cols], w, b))
                carry_ref[q, :, cols] = a_q[seg - (CONV_W - 1):]
            halves.append(pieces[0] if nsub == 1 else jnp.concatenate(pieces, axis=0))
        gate, val = halves
        act = (gate * jax.nn.sigmoid(gate) * val).astype(BF16)
        acc = acc + _dot(act, wdn_ref[start:start + width, :])
    o_ref[0] = acc
    ns_ref[...] = carry_ref[...]


def _conv_ffn(x, g, w_up, conv_w, conv_b, w_down, state, *, tm, nsub, att=None, w_o=None):
    nb, s, d = x.shape
    with_oproj = att is not None
    tok_spec = pl.BlockSpec((1, tm, d), lambda b, t: (b, t, 0))
    st_spec = pl.BlockSpec((nsub, CONV_W - 1, 2 * D_FF), lambda b, t: (b, 0, 0))
    operands = [x]
    in_specs = [tok_spec]
    if with_oproj:
        operands += [att, w_o]
        in_specs += [tok_spec, _const_spec(w_o.shape)]
    operands += [g, w_up, conv_w, conv_b, w_down, state]
    in_specs += [_const_spec(g.shape), _const_spec(w_up.shape), _const_spec(conv_w.shape),
                 _const_spec(conv_b.shape), _const_spec(w_down.shape), st_spec]
    return pl.pallas_call(
        functools.partial(_ffn_kernel, tm=tm, nsub=nsub, with_oproj=with_oproj),
        grid=(nb, s // tm),
        in_specs=in_specs,
        out_specs=[tok_spec, st_spec],
        out_shape=[jax.ShapeDtypeStruct(x.shape, F32), jax.ShapeDtypeStruct(state.shape, F32)],
        scratch_shapes=[pltpu.VMEM((nsub, CONV_W - 1, 2 * D_FF), F32)],
        compiler_params=_params(("parallel", "arbitrary")),
        name="conv_ffn",
    )(*operands)


def _head_rms(x, e_ref, et_ref, g):
    ss = _dot((x * x).astype(BF16), e_ref[...])
    r = lax.rsqrt(ss * (1.0 / HEAD_DIM) + EPS)
    rhi, rlo = _split2(r)
    rb = _dot(rhi, et_ref[...]) + _dot(rlo, et_ref[...])
    return x * rb * g


def _proj_kernel(x_ref, gkv_ref, gq_ref, wk_ref, wv_ref, wq_ref, wf_ref, bf_ref, kg_ref, qg_ref,
                 e_ref, et_ref, *rest, tm, emit_bias):
    if emit_bias:
        (tril_ref, place_ref, k_ref, v_ref, lf_ref, qb_ref, kb_ref, vb_ref, cp_ref, c_ref,
         carry_ref) = rest
    else:
        (k_ref, v_ref, lf_ref, qb_ref, kb_ref, vb_ref) = rest
    part = tm // PROJ_ROW_PARTS
    staged = []
    for pt in range(PROJ_ROW_PARTS):
        xh = _rms_hat(x_ref[0, pt * part:(pt + 1) * part, :])
        hkv = (xh * gkv_ref[...]).astype(BF16)
        hq = (xh * gq_ref[...]).astype(BF16)
        staged.append((_dot(hkv, wk_ref[...]), _dot(hkv, wv_ref[...]),
                       _dot(hkv, wf_ref[...]) + bf_ref[...],
                       _dot(hq, wq_ref[...])))
    if emit_bias:
        @pl.when(pl.program_id(1) == 0)
        def _():
            carry_ref[...] = jnp.zeros_like(carry_ref)

    for pt, (k_raw, v, f, q_raw) in enumerate(staged):
        rows = slice(pt * part, (pt + 1) * part)
        k = _head_rms(k_raw, e_ref, et_ref, kg_ref[...])
        k_ref[0, rows, :] = k
        kb_ref[0, rows, :] = k.astype(BF16)
        v_ref[0, rows, :] = v
        vb_ref[0, rows, :] = v.astype(BF16)
        lf = jax.nn.log_sigmoid(f)
        lf_ref[0, rows, :] = lf[:, :N_HEADS]
        q = _head_rms(q_raw, e_ref, et_ref, qg_ref[...])
        qb_ref[0, rows, :] = (q * (LOG2E * HEAD_DIM ** -0.5)).astype(BF16)
        if emit_bias:
            c = carry_ref[0:1, :]
            for piece in _split3(lf):
                c = c + _dot(tril_ref[...], piece)
            carry_ref[0:1, :] = c[part - 1:, :]
            c_ref[0, rows, :] = c[:, :N_HEADS]
            cp = None
            for j, piece in enumerate(_split3(c * (-LOG2E))):
                term = _dot(piece, place_ref[j])
                cp = term if cp is None else cp + term
            cp_ref[0, rows, :] = cp.astype(BF16)


def _bias_placement():
    place = np.zeros((N_BIAS_TERMS, LANES, D_MODEL), np.float32)
    for h in range(N_HEADS):
        base = (h // HEADS_PER_LANE_TILE) * LANES + (HEAD_DIM if h % HEADS_PER_LANE_TILE == 0 else 0)
        for j in range(N_BIAS_TERMS):
            place[j, h, base + j] = 1.0
    return jnp.asarray(place, BF16)


def _kv_q_proj(x, g_kv, g_q, w_k, w_v, w_q, w_f, b_f, k_g, q_g, e, et, *, tm, emit_bias):
    nb, s, d = x.shape
    tok_spec = pl.BlockSpec((1, tm, d), lambda b, t: (b, t, 0))
    lf_spec = pl.BlockSpec((1, tm, N_HEADS), lambda b, t: (b, t, 0))
    consts = [g_kv, g_q, w_k, w_v, w_q, w_f, b_f, k_g, q_g, e, et]
    out_specs = [tok_spec, tok_spec, lf_spec, tok_spec, tok_spec, tok_spec]
    out_shape = [jax.ShapeDtypeStruct(x.shape, F32), jax.ShapeDtypeStruct(x.shape, F32),
                 jax.ShapeDtypeStruct((nb, s, N_HEADS), F32),
                 jax.ShapeDtypeStruct(x.shape, BF16), jax.ShapeDtypeStruct(x.shape, BF16),
                 jax.ShapeDtypeStruct(x.shape, BF16)]
    scratch = []
    if emit_bias:
        part = tm // PROJ_ROW_PARTS
        consts += [jnp.asarray(np.tril(np.ones((part, part), np.float32)), BF16), _bias_placement()]
        out_specs += [tok_spec, lf_spec]
        out_shape += [jax.ShapeDtypeStruct(x.shape, BF16), jax.ShapeDtypeStruct((nb, s, N_HEADS), F32)]
        scratch.append(pltpu.VMEM((SUBLANES, LANES), F32))
    return pl.pallas_call(
        functools.partial(_proj_kernel, tm=tm, emit_bias=emit_bias),
        grid=(nb, s // tm),
        in_specs=[tok_spec] + [_const_spec(c.shape) for c in consts],
        out_specs=out_specs,
        out_shape=out_shape,
        scratch_shapes=scratch,
        compiler_params=_params(("parallel", "arbitrary")),
        name="kv_q_proj",
    )(x, *consts)


CUMSUM_TILE = 512


def _cumsum_kernel(x_ref, tri_ref, o_ref, carry_ref):
    @pl.when(pl.program_id(0) == 0)
    def _():
        carry_ref[...] = jnp.zeros_like(carry_ref)

    tri = tri_ref[...]
    out = carry_ref[...]
    for part in _split3(x_ref[...]):
        out = out + _dot(part, tri)
    o_ref[...] = out
    carry_ref[...] = jnp.broadcast_to(out[:, CUMSUM_TILE - 1:], out.shape)


def _cumsum_lanes(x):
    r, s = x.shape
    assert s % CUMSUM_TILE == 0, s
    tri = jnp.asarray(np.triu(np.ones((CUMSUM_TILE, CUMSUM_TILE), np.float32)), BF16)
    spec = pl.BlockSpec((r, CUMSUM_TILE), lambda t: (0, t))
    return pl.pallas_call(
        _cumsum_kernel,
        grid=(s // CUMSUM_TILE,),
        in_specs=[spec, _const_spec(tri.shape)],
        out_specs=spec,
        out_shape=jax.ShapeDtypeStruct(x.shape, F32),
        scratch_shapes=[pltpu.VMEM((r, CUMSUM_TILE), F32)],
        compiler_params=_params(("arbitrary",)),
        name="logf_cumsum",
    )(x, tri)


def _attn_kernel(skip_ref, q_ref, k_ref, v_ref, cp_ref, o_ref, *, tq, tk):
    b, pr, i = pl.program_id(0), pl.program_id(1), pl.program_id(2)
    heads = tuple(range(HEADS_PER_LANE_TILE))
    q = q_ref[0]
    lane_q = lax.broadcasted_iota(jnp.int32, q.shape, 1)
    first_q = lane_q < HEAD_DIM
    ones_hi = jnp.where((lane_q >= HEAD_DIM) & (lane_q < HEAD_DIM + N_BIAS_TERMS), 1.0, 0.0)
    ones_hi = ones_hi.astype(BF16)
    ones_lo = jnp.where(lane_q < N_BIAS_TERMS, 1.0, 0.0).astype(BF16)
    qs = (jnp.where(first_q, q, ones_hi), jnp.where(first_q, ones_lo, q))
    first_k = lax.broadcasted_iota(jnp.int32, (tk, LANES), 1) < HEAD_DIM
    q0 = pl.multiple_of(i * tq, tq)

    def other_lanes(x, fill, hh):
        return jnp.where(first_k, x, fill) if hh == 0 else jnp.where(first_k, fill, x)

    def logits(k0, row0, masked, hs):
        k = k_ref[0, pl.ds(k0, tk), :]
        cp = cp_ref[0, pl.ds(k0, tk), :]
        ss = [_dot_nt(qs[hh][row0:], other_lanes(k, cp, hh)) for hh in hs]
        if masked is not None:
            ss = [jnp.where(masked, s, NEG_BIG) for s in ss]
        return ss

    def update(states, ss, k0, row0, hs):
        v = v_ref[0, pl.ds(k0, tk), :]
        out = []
        for (m_all, acc_all), s, hh in zip(states, ss, hs):
            m, acc = m_all[row0:], acc_all[row0:]
            m_new = jnp.maximum(m, jnp.max(s, axis=-1, keepdims=True))
            p = jnp.exp2(s - m_new).astype(BF16)
            acc = jnp.exp2(m - m_new) * acc + _dot(p, other_lanes(v, jnp.ones_like(v), hh))
            if row0:
                m_new = jnp.concatenate([m_all[:row0], m_new], axis=0)
                acc = jnp.concatenate([acc_all[:row0], acc], axis=0)
            out.append((m_new, acc))
        return tuple(out)

    def chunk_group(states, starts, row0s, masks, hs):
        sss = [logits(k0, r0, mk, hs) for k0, r0, mk in zip(starts, row0s, masks)]
        for ss, k0, r0 in zip(sss, starts, row0s):
            states = update(states, ss, k0, r0, hs)
        return states

    per_block = tq // tk
    assert per_block == CHUNKS_PER_TRIP
    init = tuple((jnp.full((tq, 1), NEG_BIG, F32), jnp.zeros((tq, LANES), F32)) for _ in heads)

    def full_trip(hs):
        def body(jj, st):
            starts = [pl.multiple_of((jj * CHUNKS_PER_TRIP + u) * tk, tk)
                      for u in range(CHUNKS_PER_TRIP)]
            return chunk_group(st, starts, [0] * CHUNKS_PER_TRIP, [None] * CHUNKS_PER_TRIP, hs)
        return body

    base = ((b * pl.num_programs(1) + pr) * pl.num_programs(2) + i) * HEADS_PER_LANE_TILE
    first_trip = [jnp.minimum(skip_ref[base + hh], i) for hh in heads]
    both_from = jnp.maximum(first_trip[0], first_trip[1])
    states = list(init)
    for hh in heads:
        (states[hh],) = lax.fori_loop(first_trip[hh], both_from, full_trip((hh,)), (states[hh],))
    states = lax.fori_loop(both_from, i, full_trip(heads), tuple(states))
    masks = []
    for d in range(per_block):
        row = lax.broadcasted_iota(jnp.int32, (tq - d * tk, tk), 0)
        col = lax.broadcasted_iota(jnp.int32, (tq - d * tk, tk), 1)
        masks.append(col <= row)
    states = chunk_group(states, [pl.multiple_of(q0 + d * tk, tk) for d in range(per_block)],
                         [d * tk for d in range(per_block)], masks, heads)
    outs = [acc / pltpu.roll(acc, HEAD_DIM, 1) for (_, acc) in states]
    o_ref[0] = jnp.where(first_q, outs[0], outs[1]).astype(o_ref.dtype)


def _dead_trips(c, q_gain, k_gain, *, tq, tk):
    nb, s, h = c.shape
    bound = QK_BOUND_SLACK * LOG2E * HEAD_DIM ** 0.5 * jnp.max(jnp.abs(q_gain)) * jnp.max(jnp.abs(k_gain))
    c_end = c[:, tk - 1::tk, :]
    c_blk = c[:, ::tq, :]
    decay = LOG2E * (c_end[:, None, :, :] - c_blk[:, :, None, :])
    dead = decay > 2.0 * bound + EXP2_DEAD
    chunk_id = jnp.arange(s // tk, dtype=jnp.int32)[None, None, :, None]
    n_dead = jnp.min(jnp.where(dead, s // tk, chunk_id), axis=2)
    trips = n_dead // CHUNKS_PER_TRIP
    trips = trips.reshape(nb, s // tq, N_HEAD_PAIRS, HEADS_PER_LANE_TILE).transpose(0, 2, 1, 3)
    return trips.reshape(-1).astype(jnp.int32)


def _prompt_attention(qb, kb, vb, cp, dead_trips, *, tq, tk):
    nb, s, d = qb.shape
    q_spec = pl.BlockSpec((1, tq, LANES), lambda b, p, i, skip: (b, i, p))
    kv_spec = pl.BlockSpec((1, s, LANES), lambda b, p, i, skip: (b, 0, p))
    return pl.pallas_call(
        functools.partial(_attn_kernel, tq=tq, tk=tk),
        grid_spec=pltpu.PrefetchScalarGridSpec(
            num_scalar_prefetch=1, grid=(nb, N_HEAD_PAIRS, s // tq),
            in_specs=[q_spec, kv_spec, kv_spec, kv_spec], out_specs=q_spec),
        out_shape=jax.ShapeDtypeStruct(qb.shape, BF16),
        compiler_params=_params(("parallel", "parallel", "parallel")),
        name="fox_attention_prompt",
    )(dead_trips, qb, kb, vb, cp)


def _sample_attn_kernel(q_ref, kt_ref, vt_ref, kn_ref, vn_ref, c_ref, o_ref, *, sq, past):
    q = q_ref[0]
    lane_head = lax.broadcasted_iota(jnp.int32, q.shape, 1) // HEAD_DIM
    qst = jnp.concatenate([jnp.where(lane_head == hl, q, jnp.zeros_like(q))
                           for hl in range(HEADS_PER_MXU_K)], axis=0)
    c = c_ref[0, 0]
    c_row0 = c[:, past:past + 1]

    def bias_rows(ck):
        b = (c_row0 - ck) * LOG2E
        return jnp.concatenate([jnp.broadcast_to(b[hl:hl + 1], (sq, b.shape[1]))
                                for hl in range(HEADS_PER_MXU_K)], axis=0)

    s_old = _dot(qst, kt_ref[0].astype(BF16)) + bias_rows(c[:, :past])
    s_new = _dot_nt(qst, kn_ref[0]) + bias_rows(c[:, past:past + LANES])
    t = lax.broadcasted_iota(jnp.int32, s_new.shape, 0) % sq
    col = lax.broadcasted_iota(jnp.int32, s_new.shape, 1)
    s_new = jnp.where(col <= t, s_new, NEG_BIG)
    m = jnp.maximum(jnp.max(s_old, axis=-1, keepdims=True), jnp.max(s_new, axis=-1, keepdims=True))
    p_old = jnp.exp2(s_old - m)
    p_new = jnp.exp2(s_new - m)
    l = jnp.sum(p_old, axis=-1, keepdims=True) + jnp.sum(p_new, axis=-1, keepdims=True)
    o = _dot_nt(p_old.astype(BF16), vt_ref[0].astype(BF16)) + _dot(p_new.astype(BF16), vn_ref[0])
    o = o / l
    out = jnp.zeros(q.shape, F32)
    for hl in range(HEADS_PER_MXU_K):
        out = jnp.where(lane_head == hl, o[hl * sq:(hl + 1) * sq], out)
    o_ref[0] = out.astype(o_ref.dtype)


def _sample_attention(qb, cache_kt, cache_vt, kn_pad, vn_pad, c_t):
    nb, sq, d = qb.shape
    past = cache_kt.shape[2]
    n_grp = N_HEADS // HEADS_PER_MXU_K
    c_grp = c_t.reshape(nb, n_grp, HEADS_PER_MXU_K, c_t.shape[2])
    q_spec = pl.BlockSpec((1, sq, MXU_K), lambda b, g: (b, 0, g))
    cache_spec = pl.BlockSpec((1, MXU_K, past), lambda b, g: (b, g, 0))
    new_spec = pl.BlockSpec((1, LANES, MXU_K), lambda b, g: (b, 0, g))
    c_spec = pl.BlockSpec((1, 1, HEADS_PER_MXU_K, c_t.shape[2]), lambda b, g: (b, g, 0, 0))
    return pl.pallas_call(
        functools.partial(_sample_attn_kernel, sq=sq, past=past),
        grid=(nb, n_grp),
        in_specs=[q_spec, cache_spec, cache_spec, new_spec, new_spec, c_spec],
        out_specs=q_spec,
        out_shape=jax.ShapeDtypeStruct(qb.shape, BF16),
        compiler_params=_params(("parallel", "parallel")),
        name="fox_attention_sample",
    )(qb, cache_kt, cache_vt, kn_pad, vn_pad, c_grp)


def _row(v):
    return v.reshape(1, -1).astype(F32)


def _run_group(x, conv_in, past, wts, *, sgu_blk, tm_mix, tm_ffn, nsub, tm_proj):
    nb, s, d = x.shape
    mask = (np.arange(SGU_BLOCK)[None, :] // CHUNK) <= (np.arange(SGU_BLOCK)[:, None] // CHUNK)
    reps = SGU_BLOCK // sgu_blk
    m_blk = jnp.asarray(mask[:sgu_blk, :sgu_blk])
    ws_blk = jnp.where(m_blk[None], wts['a_w_s'][0][:, :sgu_blk, :sgu_blk], 0.0)
    eye = jnp.eye(reps, dtype=F32)
    ws = jnp.einsum('rs,gij->grisj', eye, ws_blk).reshape(SGU_GROUPS, SGU_BLOCK, SGU_BLOCK)
    bs = jnp.tile(wts['a_b_s'][0][:, :sgu_blk], (1, reps))
    bs_full = jnp.repeat(bs.T, SGU_GROUP_DIM, axis=1)
    mix = _sgu_mixer(x, _row(wts['norm_mix'][0]), wts['a_w_in'][0], _row(wts['a_ln_g'][0]),
                     _row(wts['a_ln_b'][0]), ws.astype(BF16), bs_full, wts['a_w_out'][0],
                     tm=tm_mix, emit_v=past is not None)
    if past is not None:
        h, sgu_v = mix
    else:
        h, sgu_v = mix, None

    def ffn(h, layer, att=None):
        return _conv_ffn(h, _row(wts['norm_ffn'][layer]), wts['f_w_up'][layer],
                         wts['f_conv_w'][layer], _row(wts['f_conv_b'][layer]),
                         wts['f_w_down'][layer], conv_in[layer], tm=tm_ffn, nsub=nsub,
                         att=att, w_o=None if att is None else wts['b_w_o'][0])

    h, conv0 = ffn(h, 0)

    proj = _kv_q_proj(
        h, _row(wts['kv_norm']), _row(wts['norm_mix'][1]), wts['w_k'], wts['w_v'], wts['b_w_q'][0],
        wts['w_f_pad'], wts['b_f_pad'], _row(jnp.tile(wts['k_norm_g'], N_HEADS)),
        _row(jnp.tile(wts['q_norm_g'][0], N_HEADS)), wts['head_sum'], wts['head_bcast'], tm=tm_proj,
        emit_bias=past is None)
    return h, sgu_v, conv0, proj, ffn


def kernel(x_prompt, x_sample, cache_k, cache_v, cache_logf, cache_ffn_conv, norm_mix, norm_ffn,
           a_w_in, a_ln_g, a_ln_b, a_w_s, a_b_s, a_w_out, f_w_up, f_conv_w, f_conv_b, f_w_down,
           kv_norm, w_k, w_v, k_norm_g, w_f, b_f, b_w_q, q_norm_g, b_w_o):
    batch, seq, d = x_prompt.shape
    dec_batch, dec_seq, _ = x_sample.shape
    past = cache_k.shape[1]
    head_of_col = np.arange(d) // HEAD_DIM
    head_sum = (head_of_col[:, None] == np.arange(LANES)[None, :]).astype(np.float32)
    wts = {
        'norm_mix': norm_mix, 'norm_ffn': norm_ffn,
        'a_w_in': a_w_in.astype(BF16), 'a_ln_g': a_ln_g, 'a_ln_b': a_ln_b, 'a_w_s': a_w_s,
        'a_b_s': a_b_s, 'a_w_out': a_w_out.astype(BF16),
        'f_w_up': f_w_up.astype(BF16), 'f_conv_w': f_conv_w, 'f_conv_b': f_conv_b,
        'f_w_down': f_w_down.astype(BF16),
        'kv_norm': kv_norm, 'w_k': w_k.astype(BF16), 'w_v': w_v.astype(BF16), 'k_norm_g': k_norm_g,
        'w_f_pad': jnp.pad(w_f, ((0, 0), (0, LANES - N_HEADS))).astype(BF16),
        'b_f_pad': jnp.pad(b_f, (0, LANES - N_HEADS)).reshape(1, LANES),
        'b_w_q': b_w_q.astype(BF16), 'q_norm_g': q_norm_g, 'b_w_o': b_w_o.astype(BF16),
        'head_sum': jnp.asarray(head_sum, BF16), 'head_bcast': jnp.asarray(head_sum.T, BF16),
    }

    zero_conv = jnp.zeros((DEPTH, batch, CONV_W - 1, 2 * D_FF), F32)
    h_p, _, conv_p0, (k_p, v_p, lf_p, qb, kb, vb, cp, c_p), ffn_p = _run_group(
        x_prompt, zero_conv, None, wts, sgu_blk=SGU_BLOCK, tm_mix=1024, tm_ffn=1024, nsub=1,
        tm_proj=512)
    dead = _dead_trips(c_p, q_norm_g[0], k_norm_g, tq=1024, tk=512)
    att_p = _prompt_attention(qb, kb, vb, cp, dead, tq=1024, tk=512)
    y_p, conv_p1 = ffn_p(h_p, 1, att_p)

    rows = dec_batch * dec_seq
    xs = x_sample.reshape(1, rows, d)
    h_s, sgu_v, conv_s0, (k_s, v_s, lf_s, qb, kb, vb), ffn_s = _run_group(
        xs, cache_ffn_conv, (cache_k, cache_v, cache_logf), wts, sgu_blk=min(SGU_BLOCK, dec_seq),
        tm_mix=rows, tm_ffn=rows, nsub=dec_batch, tm_proj=rows)
    lf_s = lf_s.reshape(dec_batch, dec_seq, N_HEADS)
    pad = LANES - dec_seq
    lf_all = jnp.concatenate(
        [cache_logf.astype(F32), lf_s,
         jnp.zeros((dec_batch, CUMSUM_TILE - dec_seq, N_HEADS), F32)], axis=1)
    c_s = _cumsum_lanes(lf_all.transpose(0, 2, 1).reshape(dec_batch * N_HEADS, -1))
    c_s = c_s.reshape(dec_batch, N_HEADS, -1)
    head_major = lambda a: a.transpose(0, 2, 3, 1).reshape(dec_batch, d, past)
    pad_new = lambda a: jnp.pad(a.reshape(dec_batch, dec_seq, d), ((0, 0), (0, pad), (0, 0)))
    att_s = _sample_attention(
        qb.reshape(dec_batch, dec_seq, d), head_major(cache_k), head_major(cache_v),
        pad_new(kb), pad_new(vb), c_s)
    y_s, conv_s1 = ffn_s(h_s, 1, att_s.reshape(1, rows, d))

    heads = lambda a, b, s: a.reshape(b, s, N_HEADS, HEAD_DIM)
    return (y_p, y_s.reshape(dec_batch, dec_seq, d),
            sgu_v.reshape(N_A, dec_batch, dec_seq, d),
            jnp.stack([conv_p0, conv_p1]), jnp.stack([conv_s0, conv_s1]),
            heads(k_p, batch, seq), heads(v_p, batch, seq), lf_p,
            heads(k_s, dec_batch, dec_seq), heads(v_s, dec_batch, dec_seq), lf_s)
```

```python
import functools

import jax
import jax.numpy as jnp
import numpy as np
from jax import lax
from jax.experimental import pallas as pl
from jax.experimental.pallas import tpu as pltpu

D_MODEL = 1024
DEPTH = 2
N_A = DEPTH // 2
CHUNK = 64
SGU_BLOCK = 128
SGU_GROUPS = 4
SGU_GROUP_DIM = D_MODEL // SGU_GROUPS
D_FF = 2816
CONV_W = 3
N_HEADS = 16
HEAD_DIM = D_MODEL // N_HEADS
EPS = 1e-6

LANES = 128
SUBLANES = 8
MXU_K = 256
HEADS_PER_MXU_K = MXU_K // HEAD_DIM
HEADS_PER_LANE_TILE = LANES // HEAD_DIM
N_HEAD_PAIRS = N_HEADS // HEADS_PER_LANE_TILE
NEG_BIG = -1e30
LOG2E = 1.4426950408889634
N_BIAS_TERMS = 3
QK_BOUND_SLACK = 1.05
EXP2_DEAD = 160.0
SGU_ROW_PARTS = 2
PROJ_ROW_PARTS = 2
VMEM_LIMIT = 56 * 1024 * 1024

F32 = jnp.float32
BF16 = jnp.bfloat16


def _const_spec(shape):
    nd = len(shape)
    return pl.BlockSpec(shape, lambda *_: (0,) * nd, pipeline_mode=pl.Buffered(1))


def _params(semantics):
    return pltpu.CompilerParams(dimension_semantics=semantics, vmem_limit_bytes=VMEM_LIMIT)


def _rms_hat(x):
    return x * lax.rsqrt(jnp.mean(x * x, axis=-1, keepdims=True) + EPS)


def _split2(x):
    hi = x.astype(BF16)
    lo = (x - hi.astype(F32)).astype(BF16)
    return hi, lo


def _split3(x):
    hi = x.astype(BF16)
    r = x - hi.astype(F32)
    mid = r.astype(BF16)
    lo = (r - mid.astype(F32)).astype(BF16)
    return hi, mid, lo


def _dot(a, b):
    return jnp.dot(a, b, preferred_element_type=F32)


def _dot_nt(a, b):
    return lax.dot_general(a, b, (((1,), (1,)), ((), ())), preferred_element_type=F32)


def _sgu_kernel(x_ref, g_ref, win_ref, lng_ref, lnb_ref, ws_ref, bs_ref, wout_ref,
                h_ref, *rest, tm, emit_v):
    if emit_v:
        v_ref, gated_ref = rest
    else:
        (gated_ref,) = rest
    part = tm // SGU_ROW_PARTS
    assert part % SGU_BLOCK == 0
    bias = bs_ref[...]
    staged = []
    for pt in range(SGU_ROW_PARTS):
        x = x_ref[0, pt * part:(pt + 1) * part, :]
        hn = (_rms_hat(x) * g_ref[...]).astype(BF16)
        staged.append((x, _dot(hn, win_ref[:, :D_MODEL]), _dot(hn, win_ref[:, D_MODEL:])))
    for pt, (x, zu, zv) in enumerate(staged):
        u = jax.nn.gelu(zu, approximate=True)
        v = jax.nn.gelu(zv, approximate=True)
        vc = v - jnp.mean(v, axis=-1, keepdims=True)
        v = vc * lax.rsqrt(jnp.mean(vc * vc, axis=-1, keepdims=True) + EPS)
        v = v * lng_ref[...] + lnb_ref[...]
        if emit_v:
            v_ref[0, pt * part:(pt + 1) * part, :] = v
        vb = v.astype(BF16)
        for r in range(part // SGU_BLOCK):
            rows = slice(r * SGU_BLOCK, (r + 1) * SGU_BLOCK)
            out_rows = slice(pt * part + r * SGU_BLOCK, pt * part + (r + 1) * SGU_BLOCK)
            for g in range(SGU_GROUPS):
                cols = slice(g * SGU_GROUP_DIM, (g + 1) * SGU_GROUP_DIM)
                mixed = _dot(ws_ref[g], vb[rows, cols]) + bias[:, cols]
                gated_ref[out_rows, cols] = (u[rows, cols] * mixed).astype(BF16)
        part_rows = slice(pt * part, (pt + 1) * part)
        h_ref[0, part_rows, :] = x + _dot(gated_ref[part_rows, :], wout_ref[...])


def _sgu_mixer(x, g, w_in, ln_g, ln_b, ws, bs_full, w_out, *, tm, emit_v):
    nb, s, d = x.shape
    tok_spec = pl.BlockSpec((1, tm, d), lambda b, t: (b, t, 0))
    out_shape = [jax.ShapeDtypeStruct(x.shape, F32)]
    out_specs = [tok_spec]
    if emit_v:
        out_shape.append(jax.ShapeDtypeStruct(x.shape, F32))
        out_specs.append(tok_spec)
    res = pl.pallas_call(
        functools.partial(_sgu_kernel, tm=tm, emit_v=emit_v),
        grid=(nb, s // tm),
        in_specs=[tok_spec, _const_spec(g.shape), _const_spec(w_in.shape), _const_spec(ln_g.shape),
                  _const_spec(ln_b.shape), _const_spec(ws.shape), _const_spec(bs_full.shape),
                  _const_spec(w_out.shape)],
        out_specs=out_specs,
        out_shape=out_shape,
        scratch_shapes=[pltpu.VMEM((tm, d), BF16)],
        compiler_params=_params(("parallel", "parallel")),
        name="sgu_mixer",
    )(x, g, w_in, ln_g, ln_b, ws, bs_full, w_out)
    return res if emit_v else res[0]


def _ff_chunks():
    width, chunks, start = 1024, [], 0
    while start < D_FF:
        w = min(width, D_FF - start)
        chunks.append((start, w))
        start += w
    return chunks


def _causal_conv3(a, prev, w, b):
    row = lax.broadcasted_iota(jnp.int32, (SUBLANES, a.shape[1]), 0)

    def shifted(k):
        r = pltpu.roll(a, k, 0)
        head = r[:SUBLANES]
        for j in range(k):
            head = jnp.where(row == j, prev[CONV_W - 1 - k + j:CONV_W - k + j], head)
        return jnp.concatenate([head, r[SUBLANES:]], axis=0)

    return shifted(2) * w[0:1] + shifted(1) * w[1:2] + a * w[2:3] + b


def _ffn_kernel(*refs, tm, nsub, with_oproj):
    if with_oproj:
        (x_ref, att_ref, wo_ref, g_ref, wup_ref, cw_ref, cb_ref, wdn_ref, st_ref,
         o_ref, ns_ref, carry_ref) = refs
    else:
        (x_ref, g_ref, wup_ref, cw_ref, cb_ref, wdn_ref, st_ref,
         o_ref, ns_ref, carry_ref) = refs
    seg = tm // nsub

    @pl.when(pl.program_id(1) == 0)
    def _():
        carry_ref[...] = st_ref[...]

    x = x_ref[0]
    if with_oproj:
        x = x + _dot(att_ref[0], wo_ref[...])
    hn = (_rms_hat(x) * g_ref[...]).astype(BF16)
    acc = x
    chunks = _ff_chunks()

    def up_proj(ci):
        start, width = chunks[ci]
        return [_dot(hn, wup_ref[:, off:off + width]) for off in (start, D_FF + start)]

    a_next = up_proj(0)
    for ci, (start, width) in enumerate(chunks):
        a_cur = a_next
        if ci + 1 < len(chunks):
            a_next = up_proj(ci + 1)
        halves = []
        for a, off in zip(a_cur, (start, D_FF + start)):
            cols = slice(off, off + width)
            w = cw_ref[:, cols]
            b = cb_ref[:, cols]
            pieces = []
            for q in range(nsub):
                a_q = a[q * seg:(q + 1) * seg]
                pieces.append(_causal_conv3(a_q, carry_ref[q, :, cols], w, b))
                carry_ref[q, :, cols] = a_q[seg - (CONV_W - 1):]
            halves.append(pieces[0] if nsub == 1 else jnp.concatenate(pieces, axis=0))
        gate, val = halves
        act = (gate * jax.nn.sigmoid(gate) * val).astype(BF16)
        acc = acc + _dot(act, wdn_ref[start:start + width, :])
    o_ref[0] = acc
    ns_ref[...] = carry_ref[...]


def _conv_ffn(x, g, w_up, conv_w, conv_b, w_down, state, *, tm, nsub, att=None, w_o=None):
    nb, s, d = x.shape
    with_oproj = att is not None
    tok_spec = pl.BlockSpec((1, tm, d), lambda b, t: (b, t, 0))
    st_spec = pl.BlockSpec((nsub, CONV_W - 1, 2 * D_FF), lambda b, t: (b, 0, 0))
    operands = [x]
    in_specs = [tok_spec]
    if with_oproj:
        operands += [att, w_o]
        in_specs += [tok_spec, _const_spec(w_o.shape)]
    operands += [g, w_up, conv_w, conv_b, w_down, state]
    in_specs += [_const_spec(g.shape), _const_spec(w_up.shape), _const_spec(conv_w.shape),
                 _const_spec(conv_b.shape), _const_spec(w_down.shape), st_spec]
    return pl.pallas_call(
        functools.partial(_ffn_kernel, tm=tm, nsub=nsub, with_oproj=with_oproj),
        grid=(nb, s // tm),
        in_specs=in_specs,
        out_specs=[tok_spec, st_spec],
        out_shape=[jax.ShapeDtypeStruct(x.shape, F32), jax.ShapeDtypeStruct(state.shape, F32)],
        scratch_shapes=[pltpu.VMEM((nsub, CONV_W - 1, 2 * D_FF), F32)],
        compiler_params=_params(("parallel", "arbitrary")),
        name="conv_ffn",
    )(*operands)


def _head_rms(x, e_ref, et_ref, g):
    ss = _dot((x * x).astype(BF16), e_ref[...])
    r = lax.rsqrt(ss * (1.0 / HEAD_DIM) + EPS)
    rb = _dot(jnp.concatenate(_split2(r), axis=1), et_ref[...])
    return x * rb * g


def _proj_kernel(x_ref, gkv_ref, gq_ref, wk_ref, wv_ref, wq_ref, wf_ref, bf_ref, kg_ref, qg_ref,
                 e_ref, et_ref, *rest, tm, emit_bias):
    if emit_bias:
        (tril_ref, place_ref, k_ref, v_ref, lf_ref, qb_ref, kb_ref, vb_ref, cp_ref, c_ref,
         carry_ref) = rest
    else:
        (k_ref, v_ref, lf_ref, qb_ref, kb_ref, vb_ref) = rest
    part = tm // PROJ_ROW_PARTS
    staged = []
    for pt in range(PROJ_ROW_PARTS):
        xh = _rms_hat(x_ref[0, pt * part:(pt + 1) * part, :])
        hkv = (xh * gkv_ref[...]).astype(BF16)
        hq = (xh * gq_ref[...]).astype(BF16)
        staged.append((_dot(hkv, wk_ref[...]), _dot(hkv, wv_ref[...]),
                       _dot(hkv, wf_ref[...]) + bf_ref[...],
                       _dot(hq, wq_ref[...])))
    if emit_bias:
        @pl.when(pl.program_id(1) == 0)
        def _():
            carry_ref[...] = jnp.zeros_like(carry_ref)

    for pt, (k_raw, v, f, q_raw) in enumerate(staged):
        rows = slice(pt * part, (pt + 1) * part)
        k = _head_rms(k_raw, e_ref, et_ref, kg_ref[...])
        k_ref[0, rows, :] = k
        kb_ref[0, rows, :] = k.astype(BF16)
        v_ref[0, rows, :] = v
        vb_ref[0, rows, :] = v.astype(BF16)
        lf = jax.nn.log_sigmoid(f)
        lf_ref[0, rows, :] = lf[:, :N_HEADS]
        q = _head_rms(q_raw, e_ref, et_ref, qg_ref[...])
        qb_ref[0, rows, :] = (q * (LOG2E * HEAD_DIM ** -0.5)).astype(BF16)
        if emit_bias:
            c = carry_ref[0:1, :]
            for piece in _split3(lf):
                c = c + _dot(tril_ref[...], piece)
            carry_ref[0:1, :] = c[part - 1:, :]
            c_ref[0, rows, :] = c[:, :N_HEADS]
            pieces = jnp.concatenate(_split3(c * (-LOG2E)), axis=1)
            cp_ref[0, rows, :] = _dot(pieces, place_ref[...]).astype(BF16)


def _bias_placement():
    place = np.zeros((N_BIAS_TERMS, LANES, D_MODEL), np.float32)
    for h in range(N_HEADS):
        base = (h // HEADS_PER_LANE_TILE) * LANES + (HEAD_DIM if h % HEADS_PER_LANE_TILE == 0 else 0)
        for j in range(N_BIAS_TERMS):
            place[j, h, base + j] = 1.0
    return jnp.asarray(place.reshape(N_BIAS_TERMS * LANES, D_MODEL), BF16)


def _kv_q_proj(x, g_kv, g_q, w_k, w_v, w_q, w_f, b_f, k_g, q_g, e, et, *, tm, emit_bias):
    nb, s, d = x.shape
    tok_spec = pl.BlockSpec((1, tm, d), lambda b, t: (b, t, 0))
    lf_spec = pl.BlockSpec((1, tm, N_HEADS), lambda b, t: (b, t, 0))
    consts = [g_kv, g_q, w_k, w_v, w_q, w_f, b_f, k_g, q_g, e, et]
    out_specs = [tok_spec, tok_spec, lf_spec, tok_spec, tok_spec, tok_spec]
    out_shape = [jax.ShapeDtypeStruct(x.shape, F32), jax.ShapeDtypeStruct(x.shape, F32),
                 jax.ShapeDtypeStruct((nb, s, N_HEADS), F32),
                 jax.ShapeDtypeStruct(x.shape, BF16), jax.ShapeDtypeStruct(x.shape, BF16),
                 jax.ShapeDtypeStruct(x.shape, BF16)]
    scratch = []
    if emit_bias:
        part = tm // PROJ_ROW_PARTS
        consts += [jnp.asarray(np.tril(np.ones((part, part), np.float32)), BF16), _bias_placement()]
        out_specs += [tok_spec, lf_spec]
        out_shape += [jax.ShapeDtypeStruct(x.shape, BF16), jax.ShapeDtypeStruct((nb, s, N_HEADS), F32)]
        scratch.append(pltpu.VMEM((SUBLANES, LANES), F32))
    return pl.pallas_call(
        functools.partial(_proj_kernel, tm=tm, emit_bias=emit_bias),
        grid=(nb, s // tm),
        in_specs=[tok_spec] + [_const_spec(c.shape) for c in consts],
        out_specs=out_specs,
        out_shape=out_shape,
        scratch_shapes=scratch,
        compiler_params=_params(("parallel", "arbitrary")),
        name="kv_q_proj",
    )(x, *consts)


CUMSUM_TILE = 512


def _cumsum_kernel(x_ref, tri_ref, o_ref, carry_ref):
    @pl.when(pl.program_id(0) == 0)
    def _():
        carry_ref[...] = jnp.zeros_like(carry_ref)

    tri = tri_ref[...]
    out = carry_ref[...]
    for part in _split3(x_ref[...]):
        out = out + _dot(part, tri)
    o_ref[...] = out
    carry_ref[...] = jnp.broadcast_to(out[:, CUMSUM_TILE - 1:], out.shape)


def _cumsum_lanes(x):
    r, s = x.shape
    assert s % CUMSUM_TILE == 0, s
    tri = jnp.asarray(np.triu(np.ones((CUMSUM_TILE, CUMSUM_TILE), np.float32)), BF16)
    spec = pl.BlockSpec((r, CUMSUM_TILE), lambda t: (0, t))
    return pl.pallas_call(
        _cumsum_kernel,
        grid=(s // CUMSUM_TILE,),
        in_specs=[spec, _const_spec(tri.shape)],
        out_specs=spec,
        out_shape=jax.ShapeDtypeStruct(x.shape, F32),
        scratch_shapes=[pltpu.VMEM((r, CUMSUM_TILE), F32)],
        compiler_params=_params(("arbitrary",)),
        name="logf_cumsum",
    )(x, tri)


def _attn_kernel(skip_ref, q_ref, k_ref, v_ref, cp_ref, o_ref, *, tq, tk, td):
    b, pr, i = pl.program_id(0), pl.program_id(1), pl.program_id(2)
    heads = tuple(range(HEADS_PER_LANE_TILE))
    q = q_ref[0]
    lane_q = lax.broadcasted_iota(jnp.int32, q.shape, 1)
    first_q = lane_q < HEAD_DIM
    ones_hi = jnp.where((lane_q >= HEAD_DIM) & (lane_q < HEAD_DIM + N_BIAS_TERMS), 1.0, 0.0)
    ones_hi = ones_hi.astype(BF16)
    ones_lo = jnp.where(lane_q < N_BIAS_TERMS, 1.0, 0.0).astype(BF16)
    qs = (jnp.where(first_q, q, ones_hi), jnp.where(first_q, ones_lo, q))
    q0 = pl.multiple_of(i * tq, tq)

    def other_lanes(x, fill, hh):
        first = lax.broadcasted_iota(jnp.int32, x.shape, 1) < HEAD_DIM
        return jnp.where(first, x, fill) if hh == 0 else jnp.where(first, fill, x)

    def logits(k0, width, row0, masked, hs):
        k = k_ref[0, pl.ds(k0, width), :]
        cp = cp_ref[0, pl.ds(k0, width), :]
        ss = [_dot_nt(qs[hh][row0:], other_lanes(k, cp, hh)) for hh in hs]
        if masked is not None:
            ss = [jnp.where(masked, s, NEG_BIG) for s in ss]
        return ss

    def update(states, ss, k0, width, row0, hs):
        v = v_ref[0, pl.ds(k0, width), :]
        out = []
        for (m_all, acc_all), s, hh in zip(states, ss, hs):
            m, acc = m_all[row0:], acc_all[row0:]
            m_new = jnp.maximum(m, jnp.max(s, axis=-1, keepdims=True))
            p = jnp.exp2(s - m_new).astype(BF16)
            acc = jnp.exp2(m - m_new) * acc + _dot(p, other_lanes(v, jnp.ones_like(v), hh))
            if row0:
                m_new = jnp.concatenate([m_all[:row0], m_new], axis=0)
                acc = jnp.concatenate([acc_all[:row0], acc], axis=0)
            out.append((m_new, acc))
        return tuple(out)

    def chunk_group(states, width, starts, row0s, masks, hs):
        sss = [logits(k0, width, r0, mk, hs) for k0, r0, mk in zip(starts, row0s, masks)]
        for ss, k0, r0 in zip(sss, starts, row0s):
            states = update(states, ss, k0, width, r0, hs)
        return states

    assert tq == tk
    init = tuple((jnp.full((tq, 1), NEG_BIG, F32), jnp.zeros((tq, LANES), F32)) for _ in heads)

    def full_trip(hs):
        def body(jj, st):
            return chunk_group(st, tk, [pl.multiple_of(jj * tk, tk)], [0], [None], hs)
        return body

    base = ((b * pl.num_programs(1) + pr) * pl.num_programs(2) + i) * HEADS_PER_LANE_TILE
    first_trip = [jnp.minimum(skip_ref[base + hh], i) for hh in heads]
    both_from = jnp.maximum(first_trip[0], first_trip[1])
    states = list(init)
    for hh in heads:
        (states[hh],) = lax.fori_loop(first_trip[hh], both_from, full_trip((hh,)), (states[hh],))
    states = lax.fori_loop(both_from, i, full_trip(heads), tuple(states))
    n_diag = tq // td
    masks = []
    for d in range(n_diag):
        row = lax.broadcasted_iota(jnp.int32, (tq - d * td, td), 0)
        col = lax.broadcasted_iota(jnp.int32, (tq - d * td, td), 1)
        masks.append(col <= row)
    states = chunk_group(states, td, [pl.multiple_of(q0 + d * td, td) for d in range(n_diag)],
                         [d * td for d in range(n_diag)], masks, heads)
    outs = [acc / pltpu.roll(acc, HEAD_DIM, 1) for (_, acc) in states]
    o_ref[0] = jnp.where(first_q, outs[0], outs[1]).astype(o_ref.dtype)


def _dead_trips(c, q_gain, k_gain, *, tq, tk):
    nb, s, h = c.shape
    bound = QK_BOUND_SLACK * LOG2E * HEAD_DIM ** 0.5 * jnp.max(jnp.abs(q_gain)) * jnp.max(jnp.abs(k_gain))
    c_end = c[:, tk - 1::tk, :]
    c_blk = c[:, ::tq, :]
    decay = LOG2E * (c_end[:, None, :, :] - c_blk[:, :, None, :])
    dead = decay > 2.0 * bound + EXP2_DEAD
    chunk_id = jnp.arange(s // tk, dtype=jnp.int32)[None, None, :, None]
    trips = jnp.min(jnp.where(dead, s // tk, chunk_id), axis=2)
    trips = trips.reshape(nb, s // tq, N_HEAD_PAIRS, HEADS_PER_LANE_TILE).transpose(0, 2, 1, 3)
    return trips.reshape(-1).astype(jnp.int32)


def _prompt_attention(qb, kb, vb, cp, dead_trips, *, tq, tk, td):
    nb, s, d = qb.shape
    q_spec = pl.BlockSpec((1, tq, LANES), lambda b, p, i, skip: (b, i, p))
    kv_spec = pl.BlockSpec((1, s, LANES), lambda b, p, i, skip: (b, 0, p))
    return pl.pallas_call(
        functools.partial(_attn_kernel, tq=tq, tk=tk, td=td),
        grid_spec=pltpu.PrefetchScalarGridSpec(
            num_scalar_prefetch=1, grid=(nb, N_HEAD_PAIRS, s // tq),
            in_specs=[q_spec, kv_spec, kv_spec, kv_spec], out_specs=q_spec),
        out_shape=jax.ShapeDtypeStruct(qb.shape, BF16),
        compiler_params=_params(("parallel", "parallel", "parallel")),
        name="fox_attention_prompt",
    )(dead_trips, qb, kb, vb, cp)


def _sample_attn_kernel(q_ref, kt_ref, vt_ref, kn_ref, vn_ref, c_ref, o_ref, *, sq, past):
    q = q_ref[0]
    lane_head = lax.broadcasted_iota(jnp.int32, q.shape, 1) // HEAD_DIM
    qst = jnp.concatenate([jnp.where(lane_head == hl, q, jnp.zeros_like(q))
                           for hl in range(HEADS_PER_MXU_K)], axis=0)
    c = c_ref[0, 0]
    c_row0 = c[:, past:past + 1]

    def bias_rows(ck):
        b = (c_row0 - ck) * LOG2E
        return jnp.concatenate([jnp.broadcast_to(b[hl:hl + 1], (sq, b.shape[1]))
                                for hl in range(HEADS_PER_MXU_K)], axis=0)

    s_old = _dot(qst, kt_ref[0].astype(BF16)) + bias_rows(c[:, :past])
    s_new = _dot_nt(qst, kn_ref[0]) + bias_rows(c[:, past:past + LANES])
    t = lax.broadcasted_iota(jnp.int32, s_new.shape, 0) % sq
    col = lax.broadcasted_iota(jnp.int32, s_new.shape, 1)
    s_new = jnp.where(col <= t, s_new, NEG_BIG)
    m = jnp.maximum(jnp.max(s_old, axis=-1, keepdims=True), jnp.max(s_new, axis=-1, keepdims=True))
    p_old = jnp.exp2(s_old - m)
    p_new = jnp.exp2(s_new - m)
    l = jnp.sum(p_old, axis=-1, keepdims=True) + jnp.sum(p_new, axis=-1, keepdims=True)
    o = _dot_nt(p_old.astype(BF16), vt_ref[0].astype(BF16)) + _dot(p_new.astype(BF16), vn_ref[0])
    o = o / l
    out = jnp.zeros(q.shape, F32)
    for hl in range(HEADS_PER_MXU_K):
        out = jnp.where(lane_head == hl, o[hl * sq:(hl + 1) * sq], out)
    o_ref[0] = out.astype(o_ref.dtype)


def _sample_attention(qb, cache_kt, cache_vt, kn_pad, vn_pad, c_t):
    nb, sq, d = qb.shape
    past = cache_kt.shape[2]
    n_grp = N_HEADS // HEADS_PER_MXU_K
    c_grp = c_t.reshape(nb, n_grp, HEADS_PER_MXU_K, c_t.shape[2])
    q_spec = pl.BlockSpec((1, sq, MXU_K), lambda b, g: (b, 0, g))
    cache_spec = pl.BlockSpec((1, MXU_K, past), lambda b, g: (b, g, 0))
    new_spec = pl.BlockSpec((1, LANES, MXU_K), lambda b, g: (b, 0, g))
    c_spec = pl.BlockSpec((1, 1, HEADS_PER_MXU_K, c_t.shape[2]), lambda b, g: (b, g, 0, 0))
    return pl.pallas_call(
        functools.partial(_sample_attn_kernel, sq=sq, past=past),
        grid=(nb, n_grp),
        in_specs=[q_spec, cache_spec, cache_spec, new_spec, new_spec, c_spec],
        out_specs=q_spec,
        out_shape=jax.ShapeDtypeStruct(qb.shape, BF16),
        compiler_params=_params(("parallel", "parallel")),
        name="fox_attention_sample",
    )(qb, cache_kt, cache_vt, kn_pad, vn_pad, c_grp)


def _row(v):
    return v.reshape(1, -1).astype(F32)


def _run_group(x, conv_in, past, wts, *, sgu_blk, tm_mix, tm_ffn, nsub, tm_proj):
    nb, s, d = x.shape
    mask = (np.arange(SGU_BLOCK)[None, :] // CHUNK) <= (np.arange(SGU_BLOCK)[:, None] // CHUNK)
    reps = SGU_BLOCK // sgu_blk
    m_blk = jnp.asarray(mask[:sgu_blk, :sgu_blk])
    ws_blk = jnp.where(m_blk[None], wts['a_w_s'][0][:, :sgu_blk, :sgu_blk], 0.0)
    eye = jnp.eye(reps, dtype=F32)
    ws = jnp.einsum('rs,gij->grisj', eye, ws_blk).reshape(SGU_GROUPS, SGU_BLOCK, SGU_BLOCK)
    bs = jnp.tile(wts['a_b_s'][0][:, :sgu_blk], (1, reps))
    bs_full = jnp.repeat(bs.T, SGU_GROUP_DIM, axis=1)
    mix = _sgu_mixer(x, _row(wts['norm_mix'][0]), wts['a_w_in'][0], _row(wts['a_ln_g'][0]),
                     _row(wts['a_ln_b'][0]), ws.astype(BF16), bs_full, wts['a_w_out'][0],
                     tm=tm_mix, emit_v=past is not None)
    if past is not None:
        h, sgu_v = mix
    else:
        h, sgu_v = mix, None

    def ffn(h, layer, att=None):
        return _conv_ffn(h, _row(wts['norm_ffn'][layer]), wts['f_w_up'][layer],
                         wts['f_conv_w'][layer], _row(wts['f_conv_b'][layer]),
                         wts['f_w_down'][layer], conv_in[layer], tm=tm_ffn, nsub=nsub,
                         att=att, w_o=None if att is None else wts['b_w_o'][0])

    h, conv0 = ffn(h, 0)

    proj = _kv_q_proj(
        h, _row(wts['kv_norm']), _row(wts['norm_mix'][1]), wts['w_k'], wts['w_v'], wts['b_w_q'][0],
        wts['w_f_pad'], wts['b_f_pad'], _row(jnp.tile(wts['k_norm_g'], N_HEADS)),
        _row(jnp.tile(wts['q_norm_g'][0], N_HEADS)), wts['head_sum'], wts['head_bcast'], tm=tm_proj,
        emit_bias=past is None)
    return h, sgu_v, conv0, proj, ffn


def kernel(x_prompt, x_sample, cache_k, cache_v, cache_logf, cache_ffn_conv, norm_mix, norm_ffn,
           a_w_in, a_ln_g, a_ln_b, a_w_s, a_b_s, a_w_out, f_w_up, f_conv_w, f_conv_b, f_w_down,
           kv_norm, w_k, w_v, k_norm_g, w_f, b_f, b_w_q, q_norm_g, b_w_o):
    batch, seq, d = x_prompt.shape
    dec_batch, dec_seq, _ = x_sample.shape
    past = cache_k.shape[1]
    head_of_col = np.arange(d) // HEAD_DIM
    head_sum = (head_of_col[:, None] == np.arange(LANES)[None, :]).astype(np.float32)
    wts = {
        'norm_mix': norm_mix, 'norm_ffn': norm_ffn,
        'a_w_in': a_w_in.astype(BF16), 'a_ln_g': a_ln_g, 'a_ln_b': a_ln_b, 'a_w_s': a_w_s,
        'a_b_s': a_b_s, 'a_w_out': a_w_out.astype(BF16),
        'f_w_up': f_w_up.astype(BF16), 'f_conv_w': f_conv_w, 'f_conv_b': f_conv_b,
        'f_w_down': f_w_down.astype(BF16),
        'kv_norm': kv_norm, 'w_k': w_k.astype(BF16), 'w_v': w_v.astype(BF16), 'k_norm_g': k_norm_g,
        'w_f_pad': jnp.pad(w_f, ((0, 0), (0, LANES - N_HEADS))).astype(BF16),
        'b_f_pad': jnp.pad(b_f, (0, LANES - N_HEADS)).reshape(1, LANES),
        'b_w_q': b_w_q.astype(BF16), 'q_norm_g': q_norm_g, 'b_w_o': b_w_o.astype(BF16),
        'head_sum': jnp.asarray(head_sum, BF16),
        'head_bcast': jnp.asarray(np.concatenate([head_sum.T, head_sum.T]), BF16),
    }

    zero_conv = jnp.zeros((DEPTH, batch, CONV_W - 1, 2 * D_FF), F32)
    h_p, _, conv_p0, (k_p, v_p, lf_p, qb, kb, vb, cp, c_p), ffn_p = _run_group(
        x_prompt, zero_conv, None, wts, sgu_blk=SGU_BLOCK, tm_mix=1024, tm_ffn=1024, nsub=1,
        tm_proj=512)
    dead = _dead_trips(c_p, q_norm_g[0], k_norm_g, tq=1024, tk=1024)
    att_p = _prompt_attention(qb, kb, vb, cp, dead, tq=1024, tk=1024, td=512)
    y_p, conv_p1 = ffn_p(h_p, 1, att_p)

    rows = dec_batch * dec_seq
    xs = x_sample.reshape(1, rows, d)
    h_s, sgu_v, conv_s0, (k_s, v_s, lf_s, qb, kb, vb), ffn_s = _run_group(
        xs, cache_ffn_conv, (cache_k, cache_v, cache_logf), wts, sgu_blk=min(SGU_BLOCK, dec_seq),
        tm_mix=rows, tm_ffn=rows, nsub=dec_batch, tm_proj=rows)
    lf_s = lf_s.reshape(dec_batch, dec_seq, N_HEADS)
    pad = LANES - dec_seq
    lf_all = jnp.concatenate(
        [cache_logf.astype(F32), lf_s,
         jnp.zeros((dec_batch, CUMSUM_TILE - dec_seq, N_HEADS), F32)], axis=1)
    c_s = _cumsum_lanes(lf_all.transpose(0, 2, 1).reshape(dec_batch * N_HEADS, -1))
    c_s = c_s.reshape(dec_batch, N_HEADS, -1)
    head_major = lambda a: a.transpose(0, 2, 3, 1).reshape(dec_batch, d, past)
    pad_new = lambda a: jnp.pad(a.reshape(dec_batch, dec_seq, d), ((0, 0), (0, pad), (0, 0)))
    att_s = _sample_attention(
        qb.reshape(dec_batch, dec_seq, d), head_major(cache_k), head_major(cache_v),
        pad_new(kb), pad_new(vb), c_s)
    y_s, conv_s1 = ffn_s(h_s, 1, att_s.reshape(1, rows, d))

    heads = lambda a, b, s: a.reshape(b, s, N_HEADS, HEAD_DIM)
    return (y_p, y_s.reshape(dec_batch, dec_seq, d),
            sgu_v.reshape(N_A, dec_batch, dec_seq, d),
            jnp.stack([conv_p0, conv_p1]), jnp.stack([conv_s0, conv_s1]),
            heads(k_p, batch, seq), heads(v_p, batch, seq), lf_p,
            heads(k_s, dec_batch, dec_seq), heads(v_s, dec_batch, dec_seq), lf_s)
```

```python
import functools

import jax
import jax.numpy as jnp
import numpy as np
from jax import lax
from jax.experimental import pallas as pl
from jax.experimental.pallas import tpu as pltpu

D_MODEL = 1024
DEPTH = 2
N_A = DEPTH // 2
CHUNK = 64
SGU_BLOCK = 128
SGU_GROUPS = 4
SGU_GROUP_DIM = D_MODEL // SGU_GROUPS
D_FF = 2816
CONV_W = 3
N_HEADS = 16
HEAD_DIM = D_MODEL // N_HEADS
EPS = 1e-6

LANES = 128
SUBLANES = 8
MXU_K = 256
HEADS_PER_MXU_K = MXU_K // HEAD_DIM
HEADS_PER_LANE_TILE = LANES // HEAD_DIM
N_HEAD_PAIRS = N_HEADS // HEADS_PER_LANE_TILE
NEG_BIG = -1e30
LOG2E = 1.4426950408889634
N_BIAS_TERMS = 3
QK_BOUND_SLACK = 1.05
EXP2_DEAD = 160.0
SGU_ROW_PARTS = 2
PROJ_ROW_PARTS = 2
VMEM_LIMIT = 56 * 1024 * 1024

F32 = jnp.float32
BF16 = jnp.bfloat16


def _const_spec(shape):
    nd = len(shape)
    return pl.BlockSpec(shape, lambda *_: (0,) * nd, pipeline_mode=pl.Buffered(1))


def _params(semantics):
    return pltpu.CompilerParams(dimension_semantics=semantics, vmem_limit_bytes=VMEM_LIMIT)


def _rms_hat(x):
    return x * lax.rsqrt(jnp.mean(x * x, axis=-1, keepdims=True) + EPS)


def _split2(x):
    hi = x.astype(BF16)
    lo = (x - hi.astype(F32)).astype(BF16)
    return hi, lo


def _split3(x):
    hi = x.astype(BF16)
    r = x - hi.astype(F32)
    mid = r.astype(BF16)
    lo = (r - mid.astype(F32)).astype(BF16)
    return hi, mid, lo


def _dot(a, b):
    return jnp.dot(a, b, preferred_element_type=F32)


def _dot_nt(a, b):
    return lax.dot_general(a, b, (((1,), (1,)), ((), ())), preferred_element_type=F32)


def _sgu_kernel(x_ref, g_ref, win_ref, lng_ref, lnb_ref, ws_ref, bs_ref, wout_ref,
                h_ref, *rest, tm, emit_v):
    if emit_v:
        v_ref, gated_ref = rest
    else:
        (gated_ref,) = rest
    part = tm // SGU_ROW_PARTS
    assert part % SGU_BLOCK == 0
    bias = bs_ref[...]
    staged = []
    for pt in range(SGU_ROW_PARTS):
        x = x_ref[0, pt * part:(pt + 1) * part, :]
        hn = (_rms_hat(x) * g_ref[...]).astype(BF16)
        staged.append((x, _dot(hn, win_ref[:, :D_MODEL]), _dot(hn, win_ref[:, D_MODEL:])))
    for pt, (x, zu, zv) in enumerate(staged):
        u = jax.nn.gelu(zu, approximate=True)
        v = jax.nn.gelu(zv, approximate=True)
        vc = v - jnp.mean(v, axis=-1, keepdims=True)
        v = vc * lax.rsqrt(jnp.mean(vc * vc, axis=-1, keepdims=True) + EPS)
        v = v * lng_ref[...] + lnb_ref[...]
        if emit_v:
            v_ref[0, pt * part:(pt + 1) * part, :] = v
        vb = v.astype(BF16)
        for r in range(part // SGU_BLOCK):
            rows = slice(r * SGU_BLOCK, (r + 1) * SGU_BLOCK)
            out_rows = slice(pt * part + r * SGU_BLOCK, pt * part + (r + 1) * SGU_BLOCK)
            for g in range(SGU_GROUPS):
                cols = slice(g * SGU_GROUP_DIM, (g + 1) * SGU_GROUP_DIM)
                mixed = _dot(ws_ref[g], vb[rows, cols]) + bias[:, cols]
                gated_ref[out_rows, cols] = (u[rows, cols] * mixed).astype(BF16)
        part_rows = slice(pt * part, (pt + 1) * part)
        h_ref[0, part_rows, :] = x + _dot(gated_ref[part_rows, :], wout_ref[...])


def _sgu_mixer(x, g, w_in, ln_g, ln_b, ws, bs_full, w_out, *, tm, emit_v):
    nb, s, d = x.shape
    tok_spec = pl.BlockSpec((1, tm, d), lambda b, t: (b, t, 0))
    out_shape = [jax.ShapeDtypeStruct(x.shape, F32)]
    out_specs = [tok_spec]
    if emit_v:
        out_shape.append(jax.ShapeDtypeStruct(x.shape, F32))
        out_specs.append(tok_spec)
    res = pl.pallas_call(
        functools.partial(_sgu_kernel, tm=tm, emit_v=emit_v),
        grid=(nb, s // tm),
        in_specs=[tok_spec, _const_spec(g.shape), _const_spec(w_in.shape), _const_spec(ln_g.shape),
                  _const_spec(ln_b.shape), _const_spec(ws.shape), _const_spec(bs_full.shape),
                  _const_spec(w_out.shape)],
        out_specs=out_specs,
        out_shape=out_shape,
        scratch_shapes=[pltpu.VMEM((tm, d), BF16)],
        compiler_params=_params(("parallel", "parallel")),
        name="sgu_mixer",
    )(x, g, w_in, ln_g, ln_b, ws, bs_full, w_out)
    return res if emit_v else res[0]


def _ff_chunks():
    width, chunks, start = 1024, [], 0
    while start < D_FF:
        w = min(width, D_FF - start)
        chunks.append((start, w))
        start += w
    return chunks


def _causal_conv3(a, prev, w, b):
    row = lax.broadcasted_iota(jnp.int32, (SUBLANES, a.shape[1]), 0)

    def shifted(k):
        r = pltpu.roll(a, k, 0)
        head = r[:SUBLANES]
        for j in range(k):
            head = jnp.where(row == j, prev[CONV_W - 1 - k + j:CONV_W - k + j], head)
        return jnp.concatenate([head, r[SUBLANES:]], axis=0)

    return shifted(2) * w[0:1] + shifted(1) * w[1:2] + a * w[2:3] + b


def _ffn_kernel(*refs, tm, nsub, with_oproj):
    if with_oproj:
        (x_ref, att_ref, wo_ref, g_ref, wup_ref, cw_ref, cb_ref, wdn_ref, st_ref,
         o_ref, ns_ref, carry_ref) = refs
    else:
        (x_ref, g_ref, wup_ref, cw_ref, cb_ref, wdn_ref, st_ref,
         o_ref, ns_ref, carry_ref) = refs
    seg = tm // nsub

    @pl.when(pl.program_id(1) == 0)
    def _():
        carry_ref[...] = st_ref[...]

    x = x_ref[0]
    if with_oproj:
        x = x + _dot(att_ref[0], wo_ref[...])
    hn = (_rms_hat(x) * g_ref[...]).astype(BF16)
    acc = x
    chunks = _ff_chunks()

    def up_proj(ci):
        start, width = chunks[ci]
        return [_dot(hn, wup_ref[:, off:off + width]) for off in (start, D_FF + start)]

    a_next = up_proj(0)
    for ci, (start, width) in enumerate(chunks):
        a_cur = a_next
        if ci + 1 < len(chunks):
            a_next = up_proj(ci + 1)
        halves = []
        for a, off in zip(a_cur, (start, D_FF + start)):
            cols = slice(off, off + width)
            w = cw_ref[:, cols]
            b = cb_ref[:, cols]
            pieces = []
            for q in range(nsub):
                a_q = a[q * seg:(q + 1) * seg]
                pieces.append(_causal_conv3(a_q, carry_ref[q, :, cols], w, b))
                carry_ref[q, :, cols] = a_q[seg - (CONV_W - 1):]
            halves.append(pieces[0] if nsub == 1 else jnp.concatenate(pieces, axis=0))
        gate, val = halves
        act = (gate * jax.nn.sigmoid(gate) * val).astype(BF16)
        acc = acc + _dot(act, wdn_ref[start:start + width, :])
    o_ref[0] = acc
    ns_ref[...] = carry_ref[...]


def _conv_ffn(x, g, w_up, conv_w, conv_b, w_down, state, *, tm, nsub, att=None, w_o=None):
    nb, s, d = x.shape
    with_oproj = att is not None
    tok_spec = pl.BlockSpec((1, tm, d), lambda b, t: (b, t, 0))
    st_spec = pl.BlockSpec((nsub, CONV_W - 1, 2 * D_FF), lambda b, t: (b, 0, 0))
    operands = [x]
    in_specs = [tok_spec]
    if with_oproj:
        operands += [att, w_o]
        in_specs += [tok_spec, _const_spec(w_o.shape)]
    operands += [g, w_up, conv_w, conv_b, w_down, state]
    in_specs += [_const_spec(g.shape), _const_spec(w_up.shape), _const_spec(conv_w.shape),
                 _const_spec(conv_b.shape), _const_spec(w_down.shape), st_spec]
    return pl.pallas_call(
        functools.partial(_ffn_kernel, tm=tm, nsub=nsub, with_oproj=with_oproj),
        grid=(nb, s // tm),
        in_specs=in_specs,
        out_specs=[tok_spec, st_spec],
        out_shape=[jax.ShapeDtypeStruct(x.shape, F32), jax.ShapeDtypeStruct(state.shape, F32)],
        scratch_shapes=[pltpu.VMEM((nsub, CONV_W - 1, 2 * D_FF), F32)],
        compiler_params=_params(("parallel", "arbitrary")),
        name="conv_ffn",
    )(*operands)


def _head_rms(x, e_ref, et_ref, g):
    ss = _dot((x * x).astype(BF16), e_ref[...])
    r = lax.rsqrt(ss * (1.0 / HEAD_DIM) + EPS)
    rb = _dot(jnp.concatenate(_split2(r), axis=1), et_ref[...])
    return x * rb * g


def _proj_kernel(x_ref, gkv_ref, gq_ref, wk_ref, wv_ref, wq_ref, wf_ref, bf_ref, kg_ref, qg_ref,
                 e_ref, et_ref, *rest, tm, emit_bias):
    if emit_bias:
        (tril_ref, place_ref, k_ref, v_ref, lf_ref, qb_ref, kb_ref, vb_ref, cp_ref, c_ref,
         carry_ref) = rest
    else:
        (k_ref, v_ref, lf_ref, qb_ref, kb_ref, vb_ref) = rest
    part = tm // PROJ_ROW_PARTS
    staged = []
    for pt in range(PROJ_ROW_PARTS):
        xh = _rms_hat(x_ref[0, pt * part:(pt + 1) * part, :])
        hkv = (xh * gkv_ref[...]).astype(BF16)
        hq = (xh * gq_ref[...]).astype(BF16)
        staged.append((_dot(hkv, wk_ref[...]), _dot(hkv, wv_ref[...]),
                       _dot(hkv, wf_ref[...]) + bf_ref[...],
                       _dot(hq, wq_ref[...])))
    if emit_bias:
        @pl.when(pl.program_id(1) == 0)
        def _():
            carry_ref[...] = jnp.zeros_like(carry_ref)

    for pt, (k_raw, v, f, q_raw) in enumerate(staged):
        rows = slice(pt * part, (pt + 1) * part)
        k = _head_rms(k_raw, e_ref, et_ref, kg_ref[...])
        k_ref[0, rows, :] = k
        kb_ref[0, rows, :] = k.astype(BF16)
        v_ref[0, rows, :] = v
        vb_ref[0, rows, :] = v.astype(BF16)
        lf = jax.nn.log_sigmoid(f)
        lf_ref[0, :, rows] = lf.T[:N_HEADS]
        q = _head_rms(q_raw, e_ref, et_ref, qg_ref[...])
        qb_ref[0, rows, :] = (q * (LOG2E * HEAD_DIM ** -0.5)).astype(BF16)
        if emit_bias:
            c = carry_ref[0:1, :]
            for piece in _split3(lf):
                c = c + _dot(tril_ref[...], piece)
            carry_ref[0:1, :] = c[part - 1:, :]
            c_ref[0, pt, 0:1, :] = c[0:1, :]
            c_ref[0, pt, 1:2, :] = c[part - 1:, :]
            pieces = jnp.concatenate(_split3(c * (-LOG2E)), axis=1)
            cp_ref[0, rows, :] = _dot(pieces, place_ref[...]).astype(BF16)


def _bias_placement():
    place = np.zeros((N_BIAS_TERMS, LANES, D_MODEL), np.float32)
    for h in range(N_HEADS):
        base = (h // HEADS_PER_LANE_TILE) * LANES + (HEAD_DIM if h % HEADS_PER_LANE_TILE == 0 else 0)
        for j in range(N_BIAS_TERMS):
            place[j, h, base + j] = 1.0
    return jnp.asarray(place.reshape(N_BIAS_TERMS * LANES, D_MODEL), BF16)


def _kv_q_proj(x, g_kv, g_q, w_k, w_v, w_q, w_f, b_f, k_g, q_g, e, et, *, tm, emit_bias):
    nb, s, d = x.shape
    tok_spec = pl.BlockSpec((1, tm, d), lambda b, t: (b, t, 0))
    lf_spec = pl.BlockSpec((1, N_HEADS, tm), lambda b, t: (b, 0, t))
    consts = [g_kv, g_q, w_k, w_v, w_q, w_f, b_f, k_g, q_g, e, et]
    out_specs = [tok_spec, tok_spec, lf_spec, tok_spec, tok_spec, tok_spec]
    out_shape = [jax.ShapeDtypeStruct(x.shape, F32), jax.ShapeDtypeStruct(x.shape, F32),
                 jax.ShapeDtypeStruct((nb, N_HEADS, s), F32),
                 jax.ShapeDtypeStruct(x.shape, BF16), jax.ShapeDtypeStruct(x.shape, BF16),
                 jax.ShapeDtypeStruct(x.shape, BF16)]
    scratch = []
    if emit_bias:
        part = tm // PROJ_ROW_PARTS
        consts += [jnp.asarray(np.tril(np.ones((part, part), np.float32)), BF16), _bias_placement()]
        out_specs += [tok_spec, pl.BlockSpec((1, PROJ_ROW_PARTS, 2, LANES), lambda b, t: (b, t, 0, 0))]
        out_shape += [jax.ShapeDtypeStruct(x.shape, BF16),
                      jax.ShapeDtypeStruct((nb, s // part, 2, LANES), F32)]
        scratch.append(pltpu.VMEM((SUBLANES, LANES), F32))
    return pl.pallas_call(
        functools.partial(_proj_kernel, tm=tm, emit_bias=emit_bias),
        grid=(nb, s // tm),
        in_specs=[tok_spec] + [_const_spec(c.shape) for c in consts],
        out_specs=out_specs,
        out_shape=out_shape,
        scratch_shapes=scratch,
        compiler_params=_params(("parallel", "arbitrary")),
        name="kv_q_proj",
    )(x, *consts)


CUMSUM_TILE = 512


def _cumsum_kernel(x_ref, tri_ref, o_ref, carry_ref):
    @pl.when(pl.program_id(0) == 0)
    def _():
        carry_ref[...] = jnp.zeros_like(carry_ref)

    tri = tri_ref[...]
    out = carry_ref[...]
    for part in _split3(x_ref[...]):
        out = out + _dot(part, tri)
    o_ref[...] = out
    carry_ref[...] = jnp.broadcast_to(out[:, CUMSUM_TILE - 1:], out.shape)


def _cumsum_lanes(x):
    r, s = x.shape
    assert s % CUMSUM_TILE == 0, s
    tri = jnp.asarray(np.triu(np.ones((CUMSUM_TILE, CUMSUM_TILE), np.float32)), BF16)
    spec = pl.BlockSpec((r, CUMSUM_TILE), lambda t: (0, t))
    return pl.pallas_call(
        _cumsum_kernel,
        grid=(s // CUMSUM_TILE,),
        in_specs=[spec, _const_spec(tri.shape)],
        out_specs=spec,
        out_shape=jax.ShapeDtypeStruct(x.shape, F32),
        scratch_shapes=[pltpu.VMEM((r, CUMSUM_TILE), F32)],
        compiler_params=_params(("arbitrary",)),
        name="logf_cumsum",
    )(x, tri)


def _attn_kernel(skip_ref, q_ref, k_ref, v_ref, cp_ref, o_ref, *, tq, tk, td):
    b, pr, i = pl.program_id(0), pl.program_id(1), pl.program_id(2)
    heads = tuple(range(HEADS_PER_LANE_TILE))
    q = q_ref[0]
    lane_q = lax.broadcasted_iota(jnp.int32, q.shape, 1)
    first_q = lane_q < HEAD_DIM
    ones_hi = jnp.where((lane_q >= HEAD_DIM) & (lane_q < HEAD_DIM + N_BIAS_TERMS), 1.0, 0.0)
    ones_hi = ones_hi.astype(BF16)
    ones_lo = jnp.where(lane_q < N_BIAS_TERMS, 1.0, 0.0).astype(BF16)
    qs = (jnp.where(first_q, q, ones_hi), jnp.where(first_q, ones_lo, q))
    q0 = pl.multiple_of(i * tq, tq)

    def other_lanes(x, fill, hh):
        first = lax.broadcasted_iota(jnp.int32, x.shape, 1) < HEAD_DIM
        return jnp.where(first, x, fill) if hh == 0 else jnp.where(first, fill, x)

    def logits(k0, width, row0, masked, hs):
        k = k_ref[0, pl.ds(k0, width), :]
        cp = cp_ref[0, pl.ds(k0, width), :]
        ss = [_dot_nt(qs[hh][row0:], other_lanes(k, cp, hh)) for hh in hs]
        if masked is not None:
            ss = [jnp.where(masked, s, NEG_BIG) for s in ss]
        return ss

    def update(states, ss, k0, width, row0, hs):
        v = v_ref[0, pl.ds(k0, width), :]
        out = []
        for (m_all, acc_all), s, hh in zip(states, ss, hs):
            m, acc = m_all[row0:], acc_all[row0:]
            m_new = jnp.maximum(m, jnp.max(s, axis=-1, keepdims=True))
            p = jnp.exp2(s - m_new).astype(BF16)
            acc = jnp.exp2(m - m_new) * acc + _dot(p, other_lanes(v, jnp.ones_like(v), hh))
            if row0:
                m_new = jnp.concatenate([m_all[:row0], m_new], axis=0)
                acc = jnp.concatenate([acc_all[:row0], acc], axis=0)
            out.append((m_new, acc))
        return tuple(out)

    def chunk_group(states, width, starts, row0s, masks, hs):
        sss = [logits(k0, width, r0, mk, hs) for k0, r0, mk in zip(starts, row0s, masks)]
        for ss, k0, r0 in zip(sss, starts, row0s):
            states = update(states, ss, k0, width, r0, hs)
        return states

    assert tq == tk
    init = tuple((jnp.full((tq, 1), NEG_BIG, F32), jnp.zeros((tq, LANES), F32)) for _ in heads)

    def full_trip(hs):
        def body(jj, st):
            return chunk_group(st, tk, [pl.multiple_of(jj * tk, tk)], [0], [None], hs)
        return body

    base = ((b * pl.num_programs(1) + pr) * pl.num_programs(2) + i) * HEADS_PER_LANE_TILE
    first_trip = [jnp.minimum(skip_ref[base + hh], i) for hh in heads]
    both_from = jnp.maximum(first_trip[0], first_trip[1])
    states = list(init)
    for hh in heads:
        (states[hh],) = lax.fori_loop(first_trip[hh], both_from, full_trip((hh,)), (states[hh],))
    states = lax.fori_loop(both_from, i, full_trip(heads), tuple(states))
    n_diag = tq // td
    masks = []
    for d in range(n_diag):
        row = lax.broadcasted_iota(jnp.int32, (tq - d * td, td), 0)
        col = lax.broadcasted_iota(jnp.int32, (tq - d * td, td), 1)
        masks.append(col <= row)
    states = chunk_group(states, td, [pl.multiple_of(q0 + d * td, td) for d in range(n_diag)],
                         [d * td for d in range(n_diag)], masks, heads)
    outs = [acc / pltpu.roll(acc, HEAD_DIM, 1) for (_, acc) in states]
    o_ref[0] = jnp.where(first_q, outs[0], outs[1]).astype(o_ref.dtype)


def _dead_trips(c_first, c_last, q_gain, k_gain):
    nb, nq, h = c_first.shape
    nk = c_last.shape[1]
    bound = QK_BOUND_SLACK * LOG2E * HEAD_DIM ** 0.5 * jnp.max(jnp.abs(q_gain)) * jnp.max(jnp.abs(k_gain))
    decay = LOG2E * (c_last[:, None, :, :] - c_first[:, :, None, :])
    dead = decay > 2.0 * bound + EXP2_DEAD
    chunk_id = jnp.arange(nk, dtype=jnp.int32)[None, None, :, None]
    trips = jnp.min(jnp.where(dead, nk, chunk_id), axis=2)
    trips = trips.reshape(nb, nq, N_HEAD_PAIRS, HEADS_PER_LANE_TILE).transpose(0, 2, 1, 3)
    return trips.reshape(-1).astype(jnp.int32)


def _prompt_attention(qb, kb, vb, cp, dead_trips, *, tq, tk, td):
    nb, s, d = qb.shape
    q_spec = pl.BlockSpec((1, tq, LANES), lambda b, p, i, skip: (b, i, p))
    kv_spec = pl.BlockSpec((1, s, LANES), lambda b, p, i, skip: (b, 0, p))
    return pl.pallas_call(
        functools.partial(_attn_kernel, tq=tq, tk=tk, td=td),
        grid_spec=pltpu.PrefetchScalarGridSpec(
            num_scalar_prefetch=1, grid=(nb, N_HEAD_PAIRS, s // tq),
            in_specs=[q_spec, kv_spec, kv_spec, kv_spec], out_specs=q_spec),
        out_shape=jax.ShapeDtypeStruct(qb.shape, BF16),
        compiler_params=_params(("parallel", "parallel", "parallel")),
        name="fox_attention_prompt",
    )(dead_trips, qb, kb, vb, cp)


def _sample_attn_kernel(q_ref, kt_ref, vt_ref, kn_ref, vn_ref, c_ref, o_ref, *, sq, past):
    q = q_ref[0]
    lane_head = lax.broadcasted_iota(jnp.int32, q.shape, 1) // HEAD_DIM
    qst = jnp.concatenate([jnp.where(lane_head == hl, q, jnp.zeros_like(q))
                           for hl in range(HEADS_PER_MXU_K)], axis=0)
    c = c_ref[0, 0]
    c_row0 = c[:, past:past + 1]

    def bias_rows(ck):
        b = (c_row0 - ck) * LOG2E
        return jnp.concatenate([jnp.broadcast_to(b[hl:hl + 1], (sq, b.shape[1]))
                                for hl in range(HEADS_PER_MXU_K)], axis=0)

    s_old = _dot(qst, kt_ref[0].astype(BF16)) + bias_rows(c[:, :past])
    s_new = _dot_nt(qst, kn_ref[0]) + bias_rows(c[:, past:past + LANES])
    t = lax.broadcasted_iota(jnp.int32, s_new.shape, 0) % sq
    col = lax.broadcasted_iota(jnp.int32, s_new.shape, 1)
    s_new = jnp.where(col <= t, s_new, NEG_BIG)
    m = jnp.maximum(jnp.max(s_old, axis=-1, keepdims=True), jnp.max(s_new, axis=-1, keepdims=True))
    p_old = jnp.exp2(s_old - m)
    p_new = jnp.exp2(s_new - m)
    l = jnp.sum(p_old, axis=-1, keepdims=True) + jnp.sum(p_new, axis=-1, keepdims=True)
    o = _dot_nt(p_old.astype(BF16), vt_ref[0].astype(BF16)) + _dot(p_new.astype(BF16), vn_ref[0])
    o = o / l
    out = jnp.zeros(q.shape, F32)
    for hl in range(HEADS_PER_MXU_K):
        out = jnp.where(lane_head == hl, o[hl * sq:(hl + 1) * sq], out)
    o_ref[0] = out.astype(o_ref.dtype)


def _sample_attention(qb, cache_kt, cache_vt, kn_pad, vn_pad, c_t):
    nb, sq, d = qb.shape
    past = cache_kt.shape[2]
    n_grp = N_HEADS // HEADS_PER_MXU_K
    c_grp = c_t.reshape(nb, n_grp, HEADS_PER_MXU_K, c_t.shape[2])
    q_spec = pl.BlockSpec((1, sq, MXU_K), lambda b, g: (b, 0, g))
    cache_spec = pl.BlockSpec((1, MXU_K, past), lambda b, g: (b, g, 0))
    new_spec = pl.BlockSpec((1, LANES, MXU_K), lambda b, g: (b, 0, g))
    c_spec = pl.BlockSpec((1, 1, HEADS_PER_MXU_K, c_t.shape[2]), lambda b, g: (b, g, 0, 0))
    return pl.pallas_call(
        functools.partial(_sample_attn_kernel, sq=sq, past=past),
        grid=(nb, n_grp),
        in_specs=[q_spec, cache_spec, cache_spec, new_spec, new_spec, c_spec],
        out_specs=q_spec,
        out_shape=jax.ShapeDtypeStruct(qb.shape, BF16),
        compiler_params=_params(("parallel", "parallel")),
        name="fox_attention_sample",
    )(qb, cache_kt, cache_vt, kn_pad, vn_pad, c_grp)


def _row(v):
    return v.reshape(1, -1).astype(F32)


def _run_group(x, conv_in, past, wts, *, sgu_blk, tm_mix, tm_ffn, nsub, tm_proj):
    nb, s, d = x.shape
    mask = (np.arange(SGU_BLOCK)[None, :] // CHUNK) <= (np.arange(SGU_BLOCK)[:, None] // CHUNK)
    reps = SGU_BLOCK // sgu_blk
    m_blk = jnp.asarray(mask[:sgu_blk, :sgu_blk])
    ws_blk = jnp.where(m_blk[None], wts['a_w_s'][0][:, :sgu_blk, :sgu_blk], 0.0)
    eye = jnp.eye(reps, dtype=F32)
    ws = jnp.einsum('rs,gij->grisj', eye, ws_blk).reshape(SGU_GROUPS, SGU_BLOCK, SGU_BLOCK)
    bs = jnp.tile(wts['a_b_s'][0][:, :sgu_blk], (1, reps))
    bs_full = jnp.repeat(bs.T, SGU_GROUP_DIM, axis=1)
    mix = _sgu_mixer(x, _row(wts['norm_mix'][0]), wts['a_w_in'][0], _row(wts['a_ln_g'][0]),
                     _row(wts['a_ln_b'][0]), ws.astype(BF16), bs_full, wts['a_w_out'][0],
                     tm=tm_mix, emit_v=past is not None)
    if past is not None:
        h, sgu_v = mix
    else:
        h, sgu_v = mix, None

    def ffn(h, layer, att=None):
        return _conv_ffn(h, _row(wts['norm_ffn'][layer]), wts['f_w_up'][layer],
                         wts['f_conv_w'][layer], _row(wts['f_conv_b'][layer]),
                         wts['f_w_down'][layer], conv_in[layer], tm=tm_ffn, nsub=nsub,
                         att=att, w_o=None if att is None else wts['b_w_o'][0])

    h, conv0 = ffn(h, 0)

    proj = _kv_q_proj(
        h, _row(wts['kv_norm']), _row(wts['norm_mix'][1]), wts['w_k'], wts['w_v'], wts['b_w_q'][0],
        wts['w_f_pad'], wts['b_f_pad'], _row(jnp.tile(wts['k_norm_g'], N_HEADS)),
        _row(jnp.tile(wts['q_norm_g'][0], N_HEADS)), wts['head_sum'], wts['head_bcast'], tm=tm_proj,
        emit_bias=past is None)
    return h, sgu_v, conv0, proj, ffn


def kernel(x_prompt, x_sample, cache_k, cache_v, cache_logf, cache_ffn_conv, norm_mix, norm_ffn,
           a_w_in, a_ln_g, a_ln_b, a_w_s, a_b_s, a_w_out, f_w_up, f_conv_w, f_conv_b, f_w_down,
           kv_norm, w_k, w_v, k_norm_g, w_f, b_f, b_w_q, q_norm_g, b_w_o):
    batch, seq, d = x_prompt.shape
    dec_batch, dec_seq, _ = x_sample.shape
    past = cache_k.shape[1]
    head_of_col = np.arange(d) // HEAD_DIM
    head_sum = (head_of_col[:, None] == np.arange(LANES)[None, :]).astype(np.float32)
    wts = {
        'norm_mix': norm_mix, 'norm_ffn': norm_ffn,
        'a_w_in': a_w_in.astype(BF16), 'a_ln_g': a_ln_g, 'a_ln_b': a_ln_b, 'a_w_s': a_w_s,
        'a_b_s': a_b_s, 'a_w_out': a_w_out.astype(BF16),
        'f_w_up': f_w_up.astype(BF16), 'f_conv_w': f_conv_w, 'f_conv_b': f_conv_b,
        'f_w_down': f_w_down.astype(BF16),
        'kv_norm': kv_norm, 'w_k': w_k.astype(BF16), 'w_v': w_v.astype(BF16), 'k_norm_g': k_norm_g,
        'w_f_pad': jnp.pad(w_f, ((0, 0), (0, LANES - N_HEADS))).astype(BF16),
        'b_f_pad': jnp.pad(b_f, (0, LANES - N_HEADS)).reshape(1, LANES),
        'b_w_q': b_w_q.astype(BF16), 'q_norm_g': q_norm_g, 'b_w_o': b_w_o.astype(BF16),
        'head_sum': jnp.asarray(head_sum, BF16),
        'head_bcast': jnp.asarray(np.concatenate([head_sum.T, head_sum.T]), BF16),
    }

    tm_proj_p = 512
    zero_conv = jnp.zeros((DEPTH, batch, CONV_W - 1, 2 * D_FF), F32)
    h_p, _, conv_p0, (k_p, v_p, lf_p, qb, kb, vb, cp, c_p), ffn_p = _run_group(
        x_prompt, zero_conv, None, wts, sgu_blk=SGU_BLOCK, tm_mix=1024, tm_ffn=1024, nsub=1,
        tm_proj=tm_proj_p)
    tq = tk = 1024
    parts_per_block = tq // (tm_proj_p // PROJ_ROW_PARTS)
    dead = _dead_trips(c_p[:, ::parts_per_block, 0, :N_HEADS],
                       c_p[:, parts_per_block - 1::parts_per_block, 1, :N_HEADS], q_norm_g[0], k_norm_g)
    att_p = _prompt_attention(qb, kb, vb, cp, dead, tq=tq, tk=tk, td=512)
    y_p, conv_p1 = ffn_p(h_p, 1, att_p)

    rows = dec_batch * dec_seq
    xs = x_sample.reshape(1, rows, d)
    h_s, sgu_v, conv_s0, (k_s, v_s, lf_s, qb, kb, vb), ffn_s = _run_group(
        xs, cache_ffn_conv, (cache_k, cache_v, cache_logf), wts, sgu_blk=min(SGU_BLOCK, dec_seq),
        tm_mix=rows, tm_ffn=rows, nsub=dec_batch, tm_proj=rows)
    lf_p = lf_p.transpose(0, 2, 1)
    lf_s = lf_s.transpose(0, 2, 1).reshape(dec_batch, dec_seq, N_HEADS)
    pad = LANES - dec_seq
    lf_all = jnp.concatenate(
        [cache_logf.astype(F32), lf_s,
         jnp.zeros((dec_batch, CUMSUM_TILE - dec_seq, N_HEADS), F32)], axis=1)
    c_s = _cumsum_lanes(lf_all.transpose(0, 2, 1).reshape(dec_batch * N_HEADS, -1))
    c_s = c_s.reshape(dec_batch, N_HEADS, -1)
    head_major = lambda a: a.transpose(0, 2, 3, 1).reshape(dec_batch, d, past)
    pad_new = lambda a: jnp.pad(a.reshape(dec_batch, dec_seq, d), ((0, 0), (0, pad), (0, 0)))
    att_s = _sample_attention(
        qb.reshape(dec_batch, dec_seq, d), head_major(cache_k), head_major(cache_v),
        pad_new(kb), pad_new(vb), c_s)
    y_s, conv_s1 = ffn_s(h_s, 1, att_s.reshape(1, rows, d))

    heads = lambda a, b, s: a.reshape(b, s, N_HEADS, HEAD_DIM)
    return (y_p, y_s.reshape(dec_batch, dec_seq, d),
            sgu_v.reshape(N_A, dec_batch, dec_seq, d),
            jnp.stack([conv_p0, conv_p1]), jnp.stack([conv_s0, conv_s1]),
            heads(k_p, batch, seq), heads(v_p, batch, seq), lf_p,
            heads(k_s, dec_batch, dec_seq), heads(v_s, dec_batch, dec_seq), lf_s)
```
